```python
import math
import jax, jax.numpy as jnp
from jax import lax
import numpy as np

D_MODEL = 1024
BATCH = 32
SEQ = 256
DEPTH = 4
DEC_BATCH = 8
DEC_SEQ = 1024
PAST_LEN = 512

GRID_W = 64
N_MIXERS = 3
N_ATTN = (DEPTH + 2) // 3
N_HGRN = (DEPTH + 1) // 3
N_GDN = DEPTH // 3
EPS = 1e-6

DA_HEADS = 8
DA_HD = D_MODEL // (2 * DA_HEADS)
ROPE_BASE = 10000.0
Q_BLOCK = 128

HG_HEADS = 8
HG_K = D_MODEL // HG_HEADS
HG_V = D_MODEL // HG_HEADS
HG_CHUNK = 16

GD_HEADS = 8
GD_K = D_MODEL // GD_HEADS
GD_V = D_MODEL // GD_HEADS
GD_CONV = 3
GD_CHUNK = 64

D_FF = 2816
FFN_CONV = 3

kernel_name = 'hybrid_diffusion_prefix_trunk_step'


def rmsnorm(x, g, eps=EPS):
    xf = x.astype(jnp.float32)
    y = xf * lax.rsqrt(jnp.mean(xf * xf, axis=-1, keepdims=True) + eps)
    return (y * g.astype(jnp.float32)).astype(x.dtype)


def dwconv(x, w):
    pad = w.shape[0] // 2
    return lax.conv_general_dilated(x, w[:, None, :], window_strides=(1,), padding=[(pad, pad)],
                                    dimension_numbers=('NWC', 'WIO', 'NWC'),
                                    feature_group_count=x.shape[-1])


def modulation(cond, w_mod, b_mod):
    m = jax.nn.silu(cond) @ w_mod + b_mod
    return [t[..., None, :] for t in jnp.split(m, 6, axis=-1)]


def axial_rope_tables(n, dtype):
    rows = n // GRID_W
    row = jnp.repeat(jnp.arange(rows, dtype=jnp.float32), GRID_W)
    col = jnp.tile(jnp.arange(GRID_W, dtype=jnp.float32), rows)
    nf = DA_HD // 4
    inv = ROPE_BASE ** (-jnp.arange(nf, dtype=jnp.float32) / nf)
    ar = row[:, None] * inv
    ac = col[:, None] * inv
    return tuple(t.reshape(n, 1, 1, nf).astype(dtype)
                 for t in (jnp.cos(ar), jnp.sin(ar), jnp.cos(ac), jnp.sin(ac)))


def _rotate(u, cos, sin):
    u1, u2 = jnp.split(u, 2, axis=-1)
    return jnp.concatenate([u1 * cos - u2 * sin, u2 * cos + u1 * sin], axis=-1)


def rope_2d(x, tables):
    cr, sr, cc, sc = tables
    xr, xc = jnp.split(x, 2, axis=-1)
    return jnp.concatenate([_rotate(xr, cr, sr), _rotate(xc, cc, sc)], axis=-1)


def diff_attend(q, k, v, lam):
    b, n = q.shape[0], q.shape[1]
    nb = n // Q_BLOCK
    qb = jnp.moveaxis(q.reshape((b, nb, Q_BLOCK) + q.shape[2:]), 1, 0)
    scale = DA_HD ** -0.5

    def block(qi):
        s = jnp.einsum('bqhcd,bkhcd->bhcqk', qi, k, preferred_element_type=jnp.float32) * scale
        p = jax.nn.softmax(s, axis=-1)
        a = (p[:, :, 0] - lam * p[:, :, 1]).astype(v.dtype)
        return jnp.einsum('bhqk,bkhe->bqhe', a, v)

    o = lax.map(block, qb)
    return jnp.moveaxis(o, 0, 1).reshape(b, n, DA_HEADS, 2 * DA_HD)


def diff_attn_mixer(h, w_in, lam_p, subln, w_out, layer, rope=None, ctx_k=None, ctx_v=None):
    b, n, _ = h.shape
    q, k, v = jnp.split(h @ w_in, 3, axis=-1)
    q = q.reshape(b, n, DA_HEADS, 2, DA_HD)
    k = k.reshape(b, n, DA_HEADS, 2, DA_HD)
    v = v.reshape(b, n, DA_HEADS, 2 * DA_HD)
    lam_init = 0.8 - 0.6 * math.exp(-0.3 * layer)
    lp = lam_p.astype(jnp.float32)
    lam = jnp.exp(jnp.sum(lp[0] * lp[1])) - jnp.exp(jnp.sum(lp[2] * lp[3])) + lam_init
    if ctx_k is None:
        keys, vals = k, v
        kv = (k.reshape(b, n, DA_HEADS, 2 * DA_HD), v)
    else:
        q = rope_2d(q, rope)
        keys = jnp.concatenate([ctx_k.reshape(b, -1, DA_HEADS, 2, DA_HD), rope_2d(k, rope)], axis=1)
        vals = jnp.concatenate([ctx_v, v], axis=1)
        kv = None
    o = rmsnorm(diff_attend(q, keys, vals, lam), subln, 1e-5) * (1.0 - lam_init)
    return o.reshape(b, n, D_MODEL) @ w_out, kv


def _heads(x, n_heads):
    b, n, _ = x.shape
    return x.reshape(b, n, n_heads, -1).transpose(0, 2, 1, 3).astype(jnp.float32)


def _chunks(x, chunk):
    b, hh, n = x.shape[:3]
    return jnp.moveaxis(x.reshape((b, hh, n // chunk, chunk) + x.shape[3:]), 2, 0)


def _unchunk(o):
    nc, b, hh, c, e = o.shape
    return jnp.moveaxis(o, 0, 2).reshape(b, hh, nc * c, e)


def gla_chunk_scan(q, k, v, logf, s0, chunk):
    causal = jnp.tril(jnp.ones((chunk, chunk), dtype=bool))[:, :, None]

    def step(S, inp):
        qc, kc, vc, fc = inp
        g = jnp.cumsum(fc, axis=2)
        decay = jnp.exp(jnp.where(causal, g[:, :, :, None, :] - g[:, :, None, :, :], -jnp.inf))
        scores = jnp.einsum('bhtk,bhtsk,bhsk->bhts', qc, decay, kc)
        o = jnp.einsum('bhck,bhkv->bhcv', qc * jnp.exp(g), S) + jnp.einsum('bhts,bhsv->bhtv', scores, vc)
        g_last = g[:, :, -1:, :]
        S = jnp.exp(g_last[:, :, 0, :, None]) * S + jnp.einsum('bhsk,bhsv->bhkv', kc * jnp.exp(g_last - g), vc)
        return S, o

    s_last, o = lax.scan(step, s0, tuple(_chunks(t, chunk) for t in (q, k, v, logf)))
    return _unchunk(o), s_last


def gated_delta_chunk_scan(q, k, v, log_a, beta, s0, chunk):
    dv = v.shape[-1]
    causal = jnp.tril(jnp.ones((chunk, chunk), dtype=bool))
    strict = jnp.tril(jnp.ones((chunk, chunk), dtype=bool), -1)
    eye = jnp.eye(chunk, dtype=jnp.float32)

    def step(S, inp):
        qc, kc, vc, ac, bc = inp
        g = jnp.cumsum(ac, axis=-1)
        decay = jnp.exp(jnp.where(causal, g[..., :, None] - g[..., None, :], -jnp.inf))
        kb = kc * bc[..., None]
        lhs = eye + jnp.where(strict, jnp.einsum('bhtk,bhsk->bhts', kb, kc) * decay, 0.0)
        rhs = jnp.concatenate([vc * bc[..., None], kb * jnp.exp(g)[..., None]], axis=-1)
        sol = lax.linalg.triangular_solve(lhs, rhs, left_side=True, lower=True)
        u, w = sol[..., :dv], sol[..., dv:]
        v_new = u - jnp.einsum('bhck,bhkv->bhcv', w, S)
        scores = jnp.einsum('bhtk,bhsk->bhts', qc, kc) * decay
        o = (jnp.einsum('bhck,bhkv->bhcv', qc * jnp.exp(g)[..., None], S)
             + jnp.einsum('bhts,bhsv->bhtv', scores, v_new))
        g_last = g[..., -1:]
        S = jnp.exp(g_last)[..., None] * S + jnp.einsum('bhsk,bhsv->bhkv', kc * jnp.exp(g_last - g)[..., None], v_new)
        return S, o

    s_last, o = lax.scan(step, s0, tuple(_chunks(t, chunk) for t in (q, k, v, log_a, beta)))
    return _unchunk(o), s_last


def hgrn2_mixer(h, w_in, lb, norm_g, w_out, s_fwd, s_bwd):
    b, n, _ = h.shape
    q, zf, zb, i, g = jnp.split(h @ w_in, 5, axis=-1)
    q = _heads(q, HG_HEADS) * HG_K ** -0.5
    i = _heads(i, HG_HEADS)
    lbh = lb.reshape(HG_HEADS, 1, HG_K)
    log_lb = jnp.log(lbh)
    log_ub = jnp.log1p(-lbh)

    def direction(z, s0, rev):
        logf = jnp.logaddexp(log_lb, log_ub + jax.nn.log_sigmoid(_heads(z, HG_HEADS)))
        key = -jnp.expm1(logf)
        args = (q, key, i, logf)
        if rev:
            args = tuple(jnp.flip(t, axis=2) for t in args)
        o, s_last = gla_chunk_scan(*args, s0.astype(jnp.float32), HG_CHUNK)
        return (jnp.flip(o, axis=2) if rev else o), s_last

    o_f, sf = direction(zf, s_fwd, False)
    o_b, sb = direction(zb, s_bwd, True)
    o = rmsnorm((o_f + o_b).transpose(0, 2, 1, 3), norm_g)
    o = o * jax.nn.silu(g.reshape(b, n, HG_HEADS, HG_V).astype(jnp.float32))
    return o.reshape(b, n, D_MODEL).astype(h.dtype) @ w_out, jnp.stack([sf, sb], axis=1)


def l2norm(x):
    return x * lax.rsqrt(jnp.sum(x * x, axis=-1, keepdims=True) + 1e-6)


def gdn_mixer(h, w_in, conv_w, a_log, dt_bias, norm_g, w_out, s_fwd, s_bwd):
    b, n, _ = h.shape
    proj = h @ w_in
    qkv = jax.nn.silu(dwconv(proj[..., :3 * D_MODEL], conv_w))
    gate = proj[..., 3 * D_MODEL:4 * D_MODEL]
    ab = proj[..., 4 * D_MODEL:].astype(jnp.float32).reshape(b, n, 4, GD_HEADS).transpose(2, 0, 3, 1)
    q, k, v = jnp.split(qkv, 3, axis=-1)
    q = l2norm(_heads(q, GD_HEADS)) * GD_K ** -0.5
    k = l2norm(_heads(k, GD_HEADS))
    v = _heads(v, GD_HEADS)

    def direction(d, s0, rev):
        a_rate = jnp.exp(a_log[d].astype(jnp.float32))[:, None]
        log_a = -a_rate * jax.nn.softplus(ab[d] + dt_bias[d].astype(jnp.float32)[:, None])
        beta = jax.nn.sigmoid(ab[2 + d])
        args = (q, k, v, log_a, beta)
        if rev:
            args = tuple(jnp.flip(t, axis=2) for t in args)
        o, s_last = gated_delta_chunk_scan(*args, s0.astype(jnp.float32), GD_CHUNK)
        return (jnp.flip(o, axis=2) if rev else o), s_last

    o_f, sf = direction(0, s_fwd, False)
    o_b, sb = direction(1, s_bwd, True)
    o = rmsnorm((o_f + o_b).transpose(0, 2, 1, 3), norm_g)
    o = o * jax.nn.silu(gate.reshape(b, n, GD_HEADS, GD_V).astype(jnp.float32))
    return o.reshape(b, n, D_MODEL).astype(h.dtype) @ w_out, jnp.stack([sf, sb], axis=1)


def conv_ffn(h, w_up, conv_w, conv_b, w_down):
    u = dwconv(h @ w_up, conv_w) + conv_b
    val, gte = jnp.split(u, 2, axis=-1)
    return (val * jax.nn.silu(gte)) @ w_down


def trunk(x, cond, P, rope=None, cache=None):
    ctx_mode = cache is None
    b = x.shape[0]
    lb_all = jnp.cumsum(jax.nn.softmax(P['hgrn_lb'].astype(jnp.float32), axis=0), axis=0)
    lb_all = lb_all - lb_all[0]
    attn_k, attn_v, hgrn_s, gdn_s = [], [], [], []
    for i in range(DEPTH):
        sh1, sc1, g1, sh2, sc2, g2 = modulation(cond, P['w_mod'][i], P['b_mod'][i])
        h = rmsnorm(x, P['norm_g'][i, 0]) * (1 + sc1) + sh1
        kind, j = i % N_MIXERS, i // N_MIXERS
        if kind == 0:
            ck = None if ctx_mode else cache['attn_k'][:, j]
            cv = None if ctx_mode else cache['attn_v'][:, j]
            out, kv = diff_attn_mixer(h, P['attn_w_in'][j], P['attn_lambda'][j], P['attn_subln'][j],
                                      P['attn_w_out'][j], i, rope, ck, cv)
            if ctx_mode:
                attn_k.append(kv[0])
                attn_v.append(kv[1])
        elif kind == 1:
            s0 = jnp.zeros((b, 2, HG_HEADS, HG_K, HG_V), jnp.float32) if ctx_mode else cache['hgrn'][:, j]
            out, st = hgrn2_mixer(h, P['hgrn_w_in'][j], lb_all[i], P['hgrn_norm'][j], P['hgrn_w_out'][j],
                                  s0[:, 0], s0[:, 1])
            if ctx_mode:
                hgrn_s.append(st)
        else:
            s0 = jnp.zeros((b, 2, GD_HEADS, GD_K, GD_V), jnp.float32) if ctx_mode else cache['gdn'][:, j]
            out, st = gdn_mixer(h, P['gdn_w_in'][j], P['gdn_conv'][j], P['gdn_a_log'][j], P['gdn_dt_bias'][j],
                                P['gdn_norm'][j], P['gdn_w_out'][j], s0[:, 0], s0[:, 1])
            if ctx_mode:
                gdn_s.append(st)
        x = x + g1 * out
        h = rmsnorm(x, P['norm_g'][i, 1]) * (1 + sc2) + sh2
        x = x + g2 * conv_ffn(h, P['ffn_w_up'][i], P['ffn_conv'][i], P['ffn_conv_b'][i], P['ffn_w_down'][i])
    y = rmsnorm(x, P['final_g'])
    if not ctx_mode:
        return y, None
    dt = x.dtype
    return y, (jnp.stack(attn_k, axis=1), jnp.stack(attn_v, axis=1),
               jnp.stack(hgrn_s, axis=1).astype(dt), jnp.stack(gdn_s, axis=1).astype(dt))


def setup_inputs(seed: int = 0) -> dict:
    key = jax.random.key(seed)
    keys = iter(jax.random.split(key, 40))

    def nrm(shape, scale):
        return jax.random.normal(next(keys), shape, jnp.float32) * scale

    def gain(shape):
        return 1.0 + nrm(shape, 0.02)

    D = D_MODEL
    dt = jnp.exp(jax.random.uniform(next(keys), (N_GDN, 2, GD_HEADS), jnp.float32,
                                    math.log(1e-3), math.log(1e-1)))
    a_log = jnp.log(jax.random.uniform(next(keys), (N_GDN, 2, GD_HEADS), jnp.float32, 1.0, 16.0))
    return {
        'x_prompt': nrm((BATCH, SEQ, D), 1.0),
        'x_sample': nrm((DEC_BATCH, DEC_SEQ, D), 1.0),
        'cache_attn_k': nrm((DEC_BATCH, N_ATTN, PAST_LEN, DA_HEADS, 2 * DA_HD), 1.0),
        'cache_attn_v': nrm((DEC_BATCH, N_ATTN, PAST_LEN, DA_HEADS, 2 * DA_HD), 1.0),
        'state_hgrn': nrm((DEC_BATCH, N_HGRN, 2, HG_HEADS, HG_K, HG_V), 0.3),
        'state_gdn': nrm((DEC_BATCH, N_GDN, 2, GD_HEADS, GD_K, GD_V), 0.1),
        'c': nrm((DEC_BATCH, D), 1.0),
        'c_ctx': nrm((D,), 1.0),
        'norm_g': gain((DEPTH, 2, D)),
        'w_mod': nrm((DEPTH, D, 6 * D), 0.5 * D ** -0.5),
        'b_mod': nrm((DEPTH, 6 * D), 0.02),
        'final_g': gain((D,)),
        'attn_w_in': nrm((N_ATTN, D, 3 * D), D ** -0.5),
        'attn_lambda': nrm((N_ATTN, 4, DA_HD), 0.1),
        'attn_subln': gain((N_ATTN, 2 * DA_HD)),
        'attn_w_out': nrm((N_ATTN, D, D), D ** -0.5),
        'hgrn_w_in': nrm((N_HGRN, D, 5 * D), D ** -0.5),
        'hgrn_lb': nrm((DEPTH, D), 0.5),
        'hgrn_norm': gain((N_HGRN, HG_V)),
        'hgrn_w_out': nrm((N_HGRN, D, D), D ** -0.5),
        'gdn_w_in': nrm((N_GDN, D, 4 * D + 4 * GD_HEADS), D ** -0.5),
        'gdn_conv': nrm((N_GDN, GD_CONV, 3 * D), GD_CONV ** -0.5),
        'gdn_a_log': a_log,
        'gdn_dt_bias': dt + jnp.log(-jnp.expm1(-dt)),
        'gdn_norm': gain((N_GDN, GD_V)),
        'gdn_w_out': nrm((N_GDN, D, D), D ** -0.5),
        'ffn_w_up': nrm((DEPTH, D, 2 * D_FF), D ** -0.5),
        'ffn_conv': nrm((DEPTH, FFN_CONV, 2 * D_FF), FFN_CONV ** -0.5),
        'ffn_conv_b': nrm((DEPTH, 2 * D_FF), 0.02),
        'ffn_w_down': nrm((DEPTH, D_FF, D), D_FF ** -0.5),
    }


def reference(x_prompt, x_sample, cache_attn_k, cache_attn_v, state_hgrn, state_gdn, c, c_ctx,
              norm_g, w_mod, b_mod, final_g, attn_w_in, attn_lambda, attn_subln, attn_w_out,
              hgrn_w_in, hgrn_lb, hgrn_norm, hgrn_w_out, gdn_w_in, gdn_conv, gdn_a_log, gdn_dt_bias,
              gdn_norm, gdn_w_out, ffn_w_up, ffn_conv, ffn_conv_b, ffn_w_down):
    P = dict(norm_g=norm_g, w_mod=w_mod, b_mod=b_mod, final_g=final_g,
             attn_w_in=attn_w_in, attn_lambda=attn_lambda, attn_subln=attn_subln, attn_w_out=attn_w_out,
             hgrn_w_in=hgrn_w_in, hgrn_lb=hgrn_lb, hgrn_norm=hgrn_norm, hgrn_w_out=hgrn_w_out,
             gdn_w_in=gdn_w_in, gdn_conv=gdn_conv, gdn_a_log=gdn_a_log, gdn_dt_bias=gdn_dt_bias,
             gdn_norm=gdn_norm, gdn_w_out=gdn_w_out,
             ffn_w_up=ffn_w_up, ffn_conv=ffn_conv, ffn_conv_b=ffn_conv_b, ffn_w_down=ffn_w_down)
    y_prompt, ctx_state = trunk(x_prompt, c_ctx, P)
    new_k, new_v, new_hgrn, new_gdn = ctx_state
    rope = axial_rope_tables(x_sample.shape[1], x_sample.dtype)
    cache = dict(attn_k=cache_attn_k, attn_v=cache_attn_v, hgrn=state_hgrn, gdn=state_gdn)
    y_sample, _ = trunk(x_sample, c, P, rope, cache)
    return (y_prompt, y_sample, new_k, new_v, new_hgrn, new_gdn)
```

```python
import functools
import math

import jax
import jax.numpy as jnp
import numpy as np
from jax import lax
from jax.experimental import pallas as pl
from jax.experimental.pallas import tpu as pltpu

F32 = jnp.float32
BF16 = jnp.bfloat16

D_MODEL = 1024
N_HEADS = 8
HEAD_DIM = 128
DA_HD = 64
GRID_W = 64
ROPE_BASE = 10000.0
D_FF = 2816
EPS = 1e-6
CHUNK = 128
N_LEVELS = 7
ROW_TILE = 1024
FF_TILE = 256
VMEM_LIMIT = 56 * 1024 * 1024


def _params(sem, vmem=VMEM_LIMIT):
    return pltpu.CompilerParams(dimension_semantics=sem, vmem_limit_bytes=vmem)


def _dot(a, b):
    return jnp.dot(a, b, preferred_element_type=F32)


def _dot_nt(a, b):
    return lax.dot_general(a, b, (((1,), (1,)), ((), ())), preferred_element_type=F32)


def _dot_tn(a, b):
    return lax.dot_general(a, b, (((0,), (0,)), ((), ())), preferred_element_type=F32)


def _silu(x):
    return x * jax.nn.sigmoid(x)


def _norm_mod(x, g, sc, sh):
    y = x * lax.rsqrt(jnp.mean(x * x, axis=-1, keepdims=True) + EPS)
    return (y * g) * (1.0 + sc) + sh


def _mod_kernel(c_ref, w_ref, b_ref, o_ref):
    s = _silu(c_ref[...]).astype(BF16)
    o_ref[...] = _dot(s, w_ref[...].astype(BF16)) + b_ref[...]


def modulation_all(cond, w_mod, b_mod):
    depth, d, n = w_mod.shape
    tn = 1024
    return pl.pallas_call(
        _mod_kernel,
        grid=(depth, n // tn),
        in_specs=[pl.BlockSpec((16, d), lambda l, j: (0, 0)),
                  pl.BlockSpec((None, d, tn), lambda l, j: (l, 0, j)),
                  pl.BlockSpec((None, 1, tn), lambda l, j: (l, 0, j))],
        out_specs=pl.BlockSpec((None, 16, tn), lambda l, j: (l, 0, j)),
        out_shape=jax.ShapeDtypeStruct((depth, 16, n), F32),
        compiler_params=_params(("arbitrary", "arbitrary")),
        name="modulation",
    )(cond, w_mod, b_mod.reshape(depth, 1, n))


def _proj_kernel(x_ref, g_ref, sc_ref, sh_ref, w_ref, o_ref, h_ref):
    @pl.when(pl.program_id(1) == 0)
    def _():
        h_ref[...] = _norm_mod(x_ref[...], g_ref[...], sc_ref[...], sh_ref[...]).astype(BF16)

    o_ref[...] = _dot(h_ref[...], w_ref[...])


def norm_proj(x, g, sc, sh, w, rows_per_mod, tn):
    r, d = x.shape
    n = w.shape[1]
    tm = ROW_TILE
    mod_spec = pl.BlockSpec((None, 1, d), lambda i, j: ((i * tm) // rows_per_mod, 0, 0))
    return pl.pallas_call(
        _proj_kernel,
        grid=(r // tm, n // tn),
        in_specs=[pl.BlockSpec((tm, d), lambda i, j: (i, 0)),
                  pl.BlockSpec((1, d), lambda i, j: (0, 0)),
                  mod_spec, mod_spec,
                  pl.BlockSpec((d, tn), lambda i, j: (0, j))],
        out_specs=pl.BlockSpec((tm, tn), lambda i, j: (i, j)),
        out_shape=jax.ShapeDtypeStruct((r, n), F32),
        scratch_shapes=[pltpu.VMEM((tm, d), BF16)],
        compiler_params=_params(("arbitrary", "arbitrary")),
        name="norm_proj",
    )(x, g.reshape(1, d), sc, sh, w)


def _out_proj_kernel(o_ref, w_ref, x_ref, gate_ref, y_ref):
    y_ref[...] = x_ref[...] + gate_ref[...] * _dot(o_ref[...], w_ref[...])


def out_proj_residual(o, w, x, gate, rows_per_mod):
    r, d = x.shape
    tm = ROW_TILE
    return pl.pallas_call(
        _out_proj_kernel,
        grid=(r // tm,),
        in_specs=[pl.BlockSpec((tm, d), lambda i: (i, 0)),
                  pl.BlockSpec((d, d), lambda i: (0, 0)),
                  pl.BlockSpec((tm, d), lambda i: (i, 0)),
                  pl.BlockSpec((None, 1, d), lambda i: ((i * tm) // rows_per_mod, 0, 0))],
        out_specs=pl.BlockSpec((tm, d), lambda i: (i, 0)),
        out_shape=jax.ShapeDtypeStruct((r, d), F32),
        compiler_params=_params(("arbitrary",)),
        name="out_proj",
    )(o, w, x, gate)


def _ffn_kernel(x_ref, g_ref, sc_ref, sh_ref, gate_ref, wv_ref, wg_ref, cv_ref, cg_ref, bv_ref,
                bg_ref, wd_ref, fg_ref, y_ref, h_ref, acc_ref, *, seq_len, final_norm):
    j = pl.program_id(1)

    @pl.when(j == 0)
    def _():
        h_ref[...] = _norm_mod(x_ref[...], g_ref[...], sc_ref[...], sh_ref[...]).astype(BF16)
        acc_ref[...] = jnp.zeros_like(acc_ref)

    tm = h_ref.shape[0]
    h = h_ref[...]
    pos = lax.broadcasted_iota(jnp.int32, (tm, FF_TILE), 0) & (seq_len - 1)
    first = pos == 0
    last = pos == seq_len - 1

    def conv(u, cw_ref, b_ref):
        prev = jnp.where(first, 0.0, pltpu.roll(u, 1, 0))
        nxt = jnp.where(last, 0.0, pltpu.roll(u, tm - 1, 0))
        return prev * cw_ref[0:1, :] + u * cw_ref[1:2, :] + nxt * cw_ref[2:3, :] + b_ref[...]

    val = conv(_dot(h, wv_ref[...]), cv_ref, bv_ref)
    gte = conv(_dot(h, wg_ref[...]), cg_ref, bg_ref)
    act = (val * _silu(gte)).astype(BF16)
    acc_ref[...] += _dot(act, wd_ref[...])

    @pl.when(j == pl.num_programs(1) - 1)
    def _():
        y = x_ref[...] + gate_ref[...] * acc_ref[...]
        if final_norm:
            y = y * lax.rsqrt(jnp.mean(y * y, axis=-1, keepdims=True) + EPS) * fg_ref[...]
        y_ref[...] = y


def conv_ffn_residual(x, g, sc, sh, gate, w_up, conv_w, conv_b, w_down, final_g, rows_per_mod,
                      seq_len, final_norm):
    r, d = x.shape
    tm = ROW_TILE
    nf = D_FF // FF_TILE
    mod_spec = pl.BlockSpec((None, 1, d), lambda i, j: ((i * tm) // rows_per_mod, 0, 0))
    kern = functools.partial(_ffn_kernel, seq_len=seq_len, final_norm=final_norm)
    conv_b = conv_b.reshape(1, 2 * D_FF)
    return pl.pallas_call(
        kern,
        grid=(r // tm, nf),
        in_specs=[pl.BlockSpec((tm, d), lambda i, j: (i, 0)),
                  pl.BlockSpec((1, d), lambda i, j: (0, 0)),
                  mod_spec, mod_spec, mod_spec,
                  pl.BlockSpec((d, FF_TILE), lambda i, j: (0, j)),
                  pl.BlockSpec((d, FF_TILE), lambda i, j: (0, nf + j)),
                  pl.BlockSpec((3, FF_TILE), lambda i, j: (0, j)),
                  pl.BlockSpec((3, FF_TILE), lambda i, j: (0, nf + j)),
                  pl.BlockSpec((1, FF_TILE), lambda i, j: (0, j)),
                  pl.BlockSpec((1, FF_TILE), lambda i, j: (0, nf + j)),
                  pl.BlockSpec((FF_TILE, d), lambda i, j: (j, 0)),
                  pl.BlockSpec((1, d), lambda i, j: (0, 0))],
        out_specs=pl.BlockSpec((tm, d), lambda i, j: (i, 0)),
        out_shape=jax.ShapeDtypeStruct((r, d), F32),
        scratch_shapes=[pltpu.VMEM((tm, d), BF16), pltpu.VMEM((tm, d), F32)],
        compiler_params=_params(("arbitrary", "arbitrary")),
        name="conv_ffn",
    )(x, g.reshape(1, d), sc, sh, gate, w_up, w_up, conv_w, conv_w, conv_b, conv_b, w_down,
      final_g.reshape(1, d))


def _diff_softmax_pv(q, k, v, lam):
    lane = lax.broadcasted_iota(jnp.int32, q.shape, 1)
    outs = []
    for c in range(2):
        qc = jnp.where((lane < DA_HD) == (c == 0), q, 0.0).astype(BF16)
        s = _dot_nt(qc, k)
        e = jnp.exp(s - jnp.max(s, axis=-1, keepdims=True))
        l = jnp.sum(e, axis=-1, keepdims=True)
        outs.append(_dot(e.astype(BF16), v) / l)
    return outs[0] - lam * outs[1]


def _subln(o, w, post_scale):
    return o * lax.rsqrt(jnp.mean(o * o, axis=-1, keepdims=True) + 1e-5) * w * post_scale


def _attn_ctx_kernel(lam_ref, q_ref, k_ref, v_ref, w_ref, o_ref, *, post_scale):
    lam = lam_ref[0]
    for h in range(N_HEADS):
        sl = slice(h * HEAD_DIM, (h + 1) * HEAD_DIM)
        q = q_ref[:, sl] * (DA_HD ** -0.5)
        o = _diff_softmax_pv(q, k_ref[:, sl].astype(BF16), v_ref[:, sl].astype(BF16), lam)
        o_ref[:, sl] = _subln(o, w_ref[...], post_scale).astype(BF16)


def attn_context(qkv, lam, subln_w, seq_len, post_scale):
    r = qkv.shape[0]
    d = D_MODEL
    kern = functools.partial(_attn_ctx_kernel, post_scale=post_scale)
    return pl.pallas_call(
        kern,
        grid=(r // seq_len,),
        in_specs=[pl.BlockSpec(memory_space=pltpu.SMEM),
                  pl.BlockSpec((seq_len, d), lambda b: (b, 0)),
                  pl.BlockSpec((seq_len, d), lambda b: (b, 1)),
                  pl.BlockSpec((seq_len, d), lambda b: (b, 2)),
                  pl.BlockSpec((1, HEAD_DIM), lambda b: (0, 0))],
        out_specs=pl.BlockSpec((seq_len, d), lambda b: (b, 0)),
        out_shape=jax.ShapeDtypeStruct((r, d), BF16),
        compiler_params=_params(("arbitrary",)),
        name="attn_context",
    )(lam, qkv, qkv, qkv, subln_w.reshape(1, HEAD_DIM))


def _rope(x, cos, sin):
    lane = lax.broadcasted_iota(jnp.int32, x.shape, 1)
    partner = jnp.where((lane & 16) == 0, pltpu.roll(x, HEAD_DIM - 16, 1), pltpu.roll(x, 16, 1))
    return x * cos + partner * sin


def _attn_lat_kernel(lam_ref, q_ref, k_ref, v_ref, ck_ref, cv_ref, cos_ref, sin_ref, w_ref, o_ref,
                     kcat_ref, vcat_ref, *, post_scale, q_tile):
    lam = lam_ref[0]
    past = ck_ref.shape[0]
    n = q_ref.shape[0]
    kcat_ref[0:past, :] = ck_ref[...].astype(BF16)
    vcat_ref[0:past, :] = cv_ref[...].astype(BF16)
    kcat_ref[past:past + n, :] = _rope(k_ref[...], cos_ref[...], sin_ref[...]).astype(BF16)
    vcat_ref[past:past + n, :] = v_ref[...].astype(BF16)
    k = kcat_ref[...]
    v = vcat_ref[...]
    for t in range(n // q_tile):
        rows = slice(t * q_tile, (t + 1) * q_tile)
        q = _rope(q_ref[rows, :], cos_ref[rows, :], sin_ref[rows, :]) * (DA_HD ** -0.5)
        o = _diff_softmax_pv(q, k, v, lam)
        o_ref[rows, :] = _subln(o, w_ref[...], post_scale).astype(BF16)


def attn_latent(qkv, cache_k, cache_v, layer_idx, cos, sin, lam, subln_w, seq_len, post_scale):
    r = qkv.shape[0]
    b = r // seq_len
    past = cache_k.shape[2]
    kern = functools.partial(_attn_lat_kernel, post_scale=post_scale, q_tile=256)
    blk = (seq_len, HEAD_DIM)
    cache_spec = pl.BlockSpec((None, None, past, HEAD_DIM), lambda i, h: (i, layer_idx, 0, h))
    tab_spec = pl.BlockSpec((seq_len, HEAD_DIM), lambda i, h: (0, 0))
    return pl.pallas_call(
        kern,
        grid=(b, N_HEADS),
        in_specs=[pl.BlockSpec(memory_space=pltpu.SMEM),
                  pl.BlockSpec(blk, lambda i, h: (i, h)),
                  pl.BlockSpec(blk, lambda i, h: (i, N_HEADS + h)),
                  pl.BlockSpec(blk, lambda i, h: (i, 2 * N_HEADS + h)),
                  cache_spec, cache_spec, tab_spec, tab_spec,
                  pl.BlockSpec((1, HEAD_DIM), lambda i, h: (0, 0))],
        out_specs=pl.BlockSpec(blk, lambda i, h: (i, h)),
        out_shape=jax.ShapeDtypeStruct((r, D_MODEL), BF16),
        scratch_shapes=[pltpu.VMEM((past + seq_len, HEAD_DIM), BF16),
                        pltpu.VMEM((past + seq_len, HEAD_DIM), BF16)],
        compiler_params=_params(("arbitrary", "arbitrary")),
        name="attn_latent",
    )(lam, qkv, qkv, qkv, cache_k, cache_v, cos, sin, subln_w.reshape(1, HEAD_DIM))


def _prefix_rows(x):
    row = lax.broadcasted_iota(jnp.int32, x.shape, 0)
    for j in range(N_LEVELS):
        s = 1 << j
        x = x + jnp.where(row >= s, pltpu.roll(x, s, 0), 0.0)
    return x


def _block_boundary(x, m):
    n, lanes = x.shape
    w = 2 * m
    if w >= 8:
        y = x.reshape(n // w, w, lanes)
        return jnp.broadcast_to(y[:, m - 1:m, :], y.shape).reshape(n, lanes)
    y = x.reshape(n // 8, 8, lanes)
    sub = lax.broadcasted_iota(jnp.int32, y.shape, 1)
    out = None
    for grp in range(8 // w):
        src = grp * w + m - 1
        b = jnp.broadcast_to(y[:, src:src + 1, :], y.shape)
        out = b if out is None else jnp.where(sub >= grp * w, b, out)
    return out.reshape(n, lanes)


def _level_masks():
    t = np.arange(CHUNK)
    ms = [(t[:, None] == t[None, :])]
    for j in range(1, N_LEVELS + 1):
        ms.append((t[:, None] >> j) == (t[None, :] >> j))
    return jnp.asarray(np.stack(ms).astype(np.float32))


def _gate_norm_out(o, gate, w):
    y = o * lax.rsqrt(jnp.mean(o * o, axis=-1, keepdims=True) + EPS) * w
    return (y * _silu(gate)).astype(BF16)


def _hgrn_chunk(q, z, v, lb, masks_ref, st, rev):
    e = jnp.exp(-jnp.abs(z))
    r = 1.0 / (1.0 + e)
    pos = z >= 0
    sig = jnp.where(pos, r, e * r)
    nsig = jnp.where(pos, e * r, r)
    f = lb + (1.0 - lb) * sig
    key = (1.0 - lb) * nsig
    lf = jnp.log(f)
    incl = _prefix_rows(lf)
    base = incl - lf if rev else incl
    row = lax.broadcasted_iota(jnp.int32, q.shape, 0)
    vb = v.astype(BF16)

    a = masks_ref[0] * _dot_nt(q.astype(BF16), key.astype(BF16))
    for lvl in range(N_LEVELS):
        m = 1 << lvl
        dm = base - _block_boundary(incl, m)
        upper = (row & m) != 0
        em = jnp.exp(jnp.where(upper, dm, -dm))
        qe = q * em
        ke = key * em
        if rev:
            qt = jnp.where(upper, 0.0, qe)
            kt = jnp.where(upper, ke, 0.0)
        else:
            qt = jnp.where(upper, qe, 0.0)
            kt = jnp.where(upper, 0.0, ke)
        a = a + masks_ref[lvl + 1] * _dot_nt(qt.astype(BF16), kt.astype(BF16))

    tot = incl[CHUNK - 1:CHUNK, :]
    if rev:
        e_in = jnp.exp(tot - base)
        e_out = jnp.exp(base)
    else:
        e_in = jnp.exp(incl)
        e_out = jnp.exp(tot - incl)
    o = _dot(a.astype(BF16), vb) + _dot_nt((q * e_in).astype(BF16), st.astype(BF16))
    st_new = jnp.exp(tot) * st + _dot_tn(vb, (key * e_out).astype(BF16))
    return o, st_new


def _hgrn_kernel(*refs, has_init, emit_state):
    refs = list(refs)
    q_ref, zf_ref, zb_ref, i_ref, g_ref, lb_ref, w_ref, masks_ref = refs[:8]
    refs = refs[8:]
    s0_ref = refs.pop(0) if has_init else None
    o_ref = refs.pop(0)
    s_out_ref = refs.pop(0) if emit_state else None
    of_ref, ob_ref, st_ref = refs

    n = q_ref.shape[0]
    nc = n // CHUNK
    lb = lb_ref[...]
    for d in range(2):
        st_ref[d] = s0_ref[d].T if has_init else jnp.zeros((HEAD_DIM, HEAD_DIM), F32)

    def body(c, carry):
        rf = pl.ds(pl.multiple_of(c * CHUNK, CHUNK), CHUNK)
        rb = pl.ds(pl.multiple_of((nc - 1 - c) * CHUNK, CHUNK), CHUNK)
        scale = HEAD_DIM ** -0.5
        o, s = _hgrn_chunk(q_ref[rf, :] * scale, zf_ref[rf, :], i_ref[rf, :], lb, masks_ref,
                           st_ref[0], False)
        of_ref[rf, :] = o
        st_ref[0] = s
        o, s = _hgrn_chunk(q_ref[rb, :] * scale, zb_ref[rb, :], i_ref[rb, :], lb, masks_ref,
                           st_ref[1], True)
        ob_ref[rb, :] = o
        st_ref[1] = s
        return carry

    lax.fori_loop(0, nc, body, 0)
    o_ref[...] = _gate_norm_out(of_ref[...] + ob_ref[...], g_ref[...], w_ref[...])
    if emit_state:
        for d in range(2):
            s_out_ref[d] = st_ref[d].T


def hgrn_scan(proj, lb, norm_w, masks, seq_len, state_in, layer_idx, emit_state):
    r = proj.shape[0]
    b = r // seq_len
    has_init = state_in is not None
    blk = (seq_len, HEAD_DIM)
    col = lambda k: pl.BlockSpec(blk, lambda i, h, k=k: (i, k * N_HEADS + h))
    state_spec = pl.BlockSpec((None, None, 2, None, HEAD_DIM, HEAD_DIM),
                              lambda i, h: (i, layer_idx, 0, h, 0, 0))
    in_specs = [col(0), col(1), col(2), col(3), col(4),
                pl.BlockSpec((None, 1, HEAD_DIM), lambda i, h: (h, 0, 0)),
                pl.BlockSpec((1, HEAD_DIM), lambda i, h: (0, 0)),
                pl.BlockSpec((N_LEVELS + 1, CHUNK, CHUNK), lambda i, h: (0, 0, 0))]
    args = [proj, proj, proj, proj, proj, lb.reshape(N_HEADS, 1, HEAD_DIM),
            norm_w.reshape(1, HEAD_DIM), masks]
    if has_init:
        in_specs.append(state_spec)
        args.append(state_in)
    out_specs = [pl.BlockSpec(blk, lambda i, h: (i, h))]
    out_shape = [jax.ShapeDtypeStruct((r, D_MODEL), BF16)]
    if emit_state:
        out_specs.append(pl.BlockSpec((None, None, 2, None, HEAD_DIM, HEAD_DIM),
                                      lambda i, h: (i, 0, 0, h, 0, 0)))
        out_shape.append(jax.ShapeDtypeStruct((b, 1, 2, N_HEADS, HEAD_DIM, HEAD_DIM), F32))
    kern = functools.partial(_hgrn_kernel, has_init=has_init, emit_state=emit_state)
    res = pl.pallas_call(
        kern,
        grid=(b, N_HEADS),
        in_specs=in_specs,
        out_specs=out_specs,
        out_shape=out_shape,
        scratch_shapes=[pltpu.VMEM(blk, F32), pltpu.VMEM(blk, F32),
                        pltpu.VMEM((2, HEAD_DIM, HEAD_DIM), F32)],
        compiler_params=_params(("arbitrary", "arbitrary")),
        name="hgrn_scan",
    )(*args)
    return res if emit_state else (res[0], None)


def _lane_column(x, lane_idx):
    lane = lax.broadcasted_iota(jnp.int32, x.shape, 1)
    colv = jnp.sum(jnp.where(lane == lane_idx, x, 0.0), axis=1, keepdims=True)
    return jnp.broadcast_to(colv, x.shape)


def _gdn_chunk(q, k, v, log_a, beta, kk, qk, s, masks_ref, rev):
    incl = _prefix_rows(log_a)
    tot = incl[CHUNK - 1:CHUNK, :]
    g = tot - incl + log_a if rev else incl
    g_end = g[0:1, :] if rev else tot
    row = lax.broadcasted_iota(jnp.int32, (CHUNK, CHUNK), 0)
    colm = lax.broadcasted_iota(jnp.int32, (CHUNK, CHUNK), 1)
    within = (row <= colm) if rev else (row >= colm)
    strict = (row < colm) if rev else (row > colm)
    decay = jnp.exp(jnp.where(within, g - g.T, -jnp.inf))
    a = jnp.where(strict, beta * kk * decay, 0.0)
    eg = jnp.exp(g)
    x = jnp.concatenate([v * beta, k * (beta * eg)], axis=1)
    t = masks_ref[0]
    for lvl in range(N_LEVELS):
        l = (a * (masks_ref[lvl + 1] - masks_ref[lvl])).astype(BF16)
        tb = t.astype(BF16)
        t = t - _dot(tb, _dot(l, tb).astype(BF16))
    x = _dot(t.astype(BF16), x.astype(BF16))
    u = x[:, :HEAD_DIM]
    w = x[:, HEAD_DIM:]
    sb = s.astype(BF16)
    v_new = u - _dot(w.astype(BF16), sb)
    vnb = v_new.astype(BF16)
    o = _dot((q * eg).astype(BF16), sb) + _dot((qk * decay).astype(BF16), vnb)
    s_new = jnp.exp(g_end) * s + _dot_tn((k * jnp.exp(g_end - g)).astype(BF16), vnb)
    return o, s_new


def _gdn_kernel(*refs, has_init, emit_state):
    refs = list(refs)
    (q_ref, k_ref, v_ref, gate_ref, ab_ref, cq_ref, ck_ref, cv_ref, rate_ref, bias_ref,
     w_ref, masks_ref) = refs[:12]
    refs = refs[12:]
    s0_ref = refs.pop(0) if has_init else None
    o_ref = refs.pop(0)
    s_out_ref = refs.pop(0) if emit_state else None
    qn_ref, kn_ref, vn_ref, of_ref, ob_ref, st_ref = refs

    n = q_ref.shape[0]
    nc = n // CHUNK
    h = pl.program_id(1)
    pos = lax.broadcasted_iota(jnp.int32, (n, HEAD_DIM), 0)

    def conv_silu(x_ref, cw_ref):
        x = x_ref[...]
        prev = jnp.where(pos == 0, 0.0, pltpu.roll(x, 1, 0))
        nxt = jnp.where(pos == n - 1, 0.0, pltpu.roll(x, n - 1, 0))
        return _silu(prev * cw_ref[0:1, :] + x * cw_ref[1:2, :] + nxt * cw_ref[2:3, :])

    def l2norm(x):
        return x * lax.rsqrt(jnp.sum(x * x, axis=-1, keepdims=True) + 1e-6)

    qn_ref[...] = l2norm(conv_silu(q_ref, cq_ref)) * (HEAD_DIM ** -0.5)
    kn_ref[...] = l2norm(conv_silu(k_ref, ck_ref))
    vn_ref[...] = conv_silu(v_ref, cv_ref)
    for d in range(2):
        st_ref[d] = s0_ref[d] if has_init else jnp.zeros((HEAD_DIM, HEAD_DIM), F32)

    def direction(rows, d):
        q = qn_ref[rows, :]
        k = kn_ref[rows, :]
        v = vn_ref[rows, :]
        ab = ab_ref[rows, :]
        la_all = -rate_ref[...] * jax.nn.softplus(ab + bias_ref[...])
        log_a = _lane_column(la_all, d * N_HEADS + h)
        beta = _lane_column(jax.nn.sigmoid(ab), (2 + d) * N_HEADS + h)
        kb = k.astype(BF16)
        kk = _dot_nt(kb, kb)
        qk = _dot_nt(q.astype(BF16), kb)
        return _gdn_chunk(q, k, v, log_a, beta, kk, qk, st_ref[d], masks_ref, d == 1)

    def body(c, carry):
        rf = pl.ds(pl.multiple_of(c * CHUNK, CHUNK), CHUNK)
        rb = pl.ds(pl.multiple_of((nc - 1 - c) * CHUNK, CHUNK), CHUNK)
        o, s = direction(rf, 0)
        of_ref[rf, :] = o
        st_ref[0] = s
        o, s = direction(rb, 1)
        ob_ref[rb, :] = o
        st_ref[1] = s
        return carry

    lax.fori_loop(0, nc, body, 0)
    o_ref[...] = _gate_norm_out(of_ref[...] + ob_ref[...], gate_ref[...], w_ref[...])
    if emit_state:
        for d in range(2):
            s_out_ref[d] = st_ref[d]


def gdn_scan(proj, ab, conv_w, rate, bias, norm_w, masks, seq_len, state_in, layer_idx,
             emit_state):
    r = proj.shape[0]
    b = r // seq_len
    has_init = state_in is not None
    blk = (seq_len, HEAD_DIM)
    col = lambda k: pl.BlockSpec(blk, lambda i, h, k=k: (i, k * N_HEADS + h))
    ccol = lambda k: pl.BlockSpec((3, HEAD_DIM), lambda i, h, k=k: (0, k * N_HEADS + h))
    vec = pl.BlockSpec((1, HEAD_DIM), lambda i, h: (0, 0))
    state_spec = pl.BlockSpec((None, None, 2, None, HEAD_DIM, HEAD_DIM),
                              lambda i, h: (i, layer_idx, 0, h, 0, 0))
    in_specs = [col(0), col(1), col(2), col(3), pl.BlockSpec(blk, lambda i, h: (i, 0)),
                ccol(0), ccol(1), ccol(2), vec, vec, vec,
                pl.BlockSpec((N_LEVELS + 1, CHUNK, CHUNK), lambda i, h: (0, 0, 0))]
    args = [proj, proj, proj, proj, ab, conv_w, conv_w, conv_w, rate, bias,
            norm_w.reshape(1, HEAD_DIM), masks]
    if has_init:
        in_specs.append(state_spec)
        args.append(state_in)
    out_specs = [pl.BlockSpec(blk, lambda i, h: (i, h))]
    out_shape = [jax.ShapeDtypeStruct((r, D_MODEL), BF16)]
    if emit_state:
        out_specs.append(pl.BlockSpec((None, None, 2, None, HEAD_DIM, HEAD_DIM),
                                      lambda i, h: (i, 0, 0, h, 0, 0)))
        out_shape.append(jax.ShapeDtypeStruct((b, 1, 2, N_HEADS, HEAD_DIM, HEAD_DIM), F32))
    kern = functools.partial(_gdn_kernel, has_init=has_init, emit_state=emit_state)
    res = pl.pallas_call(
        kern,
        grid=(b, N_HEADS),
        in_specs=in_specs,
        out_specs=out_specs,
        out_shape=out_shape,
        scratch_shapes=[pltpu.VMEM(blk, F32)] * 5 + [pltpu.VMEM((2, HEAD_DIM, HEAD_DIM), F32)],
        compiler_params=_params(("arbitrary", "arbitrary")),
        name="gdn_scan",
    )(*args)
    return res if emit_state else (res[0], None)


def _rope_tables(n):
    rows = n // GRID_W
    row = jnp.repeat(jnp.arange(rows, dtype=F32), GRID_W)
    colp = jnp.tile(jnp.arange(GRID_W, dtype=F32), rows)
    nf = DA_HD // 4
    inv = ROPE_BASE ** (-jnp.arange(nf, dtype=F32) / nf)
    ar = row[:, None] * inv
    ac = colp[:, None] * inv
    cr, sr, cc, sc = jnp.cos(ar), jnp.sin(ar), jnp.cos(ac), jnp.sin(ac)
    cos = jnp.concatenate([cr, cr, cc, cc] * 2, axis=1)
    sin = jnp.concatenate([-sr, sr, -sc, sc] * 2, axis=1)
    return cos, sin


def _trunk(x, mods, rows_per_mod, seq_len, P, cache):
    ctx_mode = cache is None
    depth = P['w_mod'].shape[0]
    new_k, new_v, hgrn_state, gdn_state = [], [], None, None
    for i in range(depth):
        sh1, sc1, g1, sh2, sc2, g2 = mods[i]
        kind, j = i % 3, i // 3
        if kind == 0:
            qkv = norm_proj(x, P['norm_g'][i, 0], sc1, sh1, P['attn_w_in'][j], rows_per_mod, 1024)
            lam_init = 0.8 - 0.6 * math.exp(-0.3 * i)
            lp = P['attn_lambda'][j]
            lam = (jnp.exp(jnp.sum(lp[0] * lp[1])) - jnp.exp(jnp.sum(lp[2] * lp[3]))
                   + lam_init).reshape(1)
            if ctx_mode:
                o = attn_context(qkv, lam, P['attn_subln'][j], seq_len, 1.0 - lam_init)
                new_k.append(qkv[:, D_MODEL:2 * D_MODEL])
                new_v.append(qkv[:, 2 * D_MODEL:])
            else:
                o = attn_latent(qkv, cache['attn_k'], cache['attn_v'], j, cache['cos'],
                                cache['sin'], lam, P['attn_subln'][j], seq_len, 1.0 - lam_init)
            w_out = P['attn_w_out'][j]
        elif kind == 1:
            proj = norm_proj(x, P['norm_g'][i, 0], sc1, sh1, P['hgrn_w_in'][j], rows_per_mod, 1024)
            o, st = hgrn_scan(proj, P['lb_all'][i], P['hgrn_norm'][j], P['masks'], seq_len,
                              None if ctx_mode else cache['hgrn'], j, ctx_mode)
            if ctx_mode:
                hgrn_state = st
            w_out = P['hgrn_w_out'][j]
        else:
            proj = norm_proj(x, P['norm_g'][i, 0], sc1, sh1, P['gdn_w_main'][j], rows_per_mod, 1024)
            ab = norm_proj(x, P['norm_g'][i, 0], sc1, sh1, P['gdn_w_tail'][j], rows_per_mod,
                           HEAD_DIM)
            o, st = gdn_scan(proj, ab, P['gdn_conv'][j], P['gdn_rate'][j], P['gdn_bias'][j],
                             P['gdn_norm'][j], P['masks'], seq_len,
                             None if ctx_mode else cache['gdn'], j, ctx_mode)
            if ctx_mode:
                gdn_state = st
            w_out = P['gdn_w_out'][j]
        x = out_proj_residual(o, w_out, x, g1, rows_per_mod)
        x = conv_ffn_residual(x, P['norm_g'][i, 1], sc2, sh2, g2, P['ffn_w_up'][i],
                              P['ffn_conv'][i], P['ffn_conv_b'][i], P['ffn_w_down'][i],
                              P['final_g'], rows_per_mod, seq_len, i == depth - 1)
    return x, (new_k, new_v, hgrn_state, gdn_state)


def kernel(x_prompt, x_sample, cache_attn_k, cache_attn_v, state_hgrn, state_gdn, c, c_ctx, norm_g, w_mod, b_mod, final_g, attn_w_in, attn_lambda, attn_subln, attn_w_out, hgrn_w_in, hgrn_lb, hgrn_norm, hgrn_w_out, gdn_w_in, gdn_conv, gdn_a_log, gdn_dt_bias, gdn_norm, gdn_w_out, ffn_w_up, ffn_conv, ffn_conv_b, ffn_w_down):
    batch, seq, d = x_prompt.shape
    dec_batch, dec_seq, _ = x_sample.shape
    depth = w_mod.shape[0]
    n_gdn = gdn_w_in.shape[0]

    lb_all = jnp.cumsum(jax.nn.softmax(hgrn_lb, axis=0), axis=0)
    lb_all = lb_all - lb_all[0]

    pad = jnp.zeros((n_gdn, HEAD_DIM - 2 * N_HEADS), F32)
    gdn_rate = jnp.concatenate([jnp.exp(gdn_a_log).reshape(n_gdn, 2 * N_HEADS), pad], axis=1)
    gdn_bias = jnp.concatenate([gdn_dt_bias.reshape(n_gdn, 2 * N_HEADS), pad], axis=1)
    tail = gdn_w_in[:, :, 4 * d:]
    tail = jnp.concatenate([tail, jnp.zeros((n_gdn, d, HEAD_DIM - tail.shape[2]), F32)], axis=2)

    P = dict(norm_g=norm_g, w_mod=w_mod, final_g=final_g,
             attn_w_in=attn_w_in.astype(BF16), attn_lambda=attn_lambda, attn_subln=attn_subln,
             attn_w_out=attn_w_out.astype(BF16),
             hgrn_w_in=hgrn_w_in.astype(BF16), lb_all=lb_all, hgrn_norm=hgrn_norm,
             hgrn_w_out=hgrn_w_out.astype(BF16),
             gdn_w_main=gdn_w_in[:, :, :4 * d].astype(BF16), gdn_w_tail=tail.astype(BF16),
             gdn_conv=gdn_conv, gdn_rate=gdn_rate.reshape(n_gdn, 1, HEAD_DIM),
             gdn_bias=gdn_bias.reshape(n_gdn, 1, HEAD_DIM), gdn_norm=gdn_norm,
             gdn_w_out=gdn_w_out.astype(BF16),
             ffn_w_up=ffn_w_up.astype(BF16), ffn_conv=ffn_conv, ffn_conv_b=ffn_conv_b,
             ffn_w_down=ffn_w_down.astype(BF16), masks=_level_masks())

    cond = jnp.zeros((16, d), F32).at[:dec_batch].set(c).at[dec_batch].set(c_ctx)
    mod = modulation_all(cond, w_mod, b_mod).reshape(depth, 16, 6, 1, d)
    mods_lat = [[mod[i, :dec_batch, t] for t in range(6)] for i in range(depth)]
    mods_ctx = [[mod[i, dec_batch:dec_batch + 1, t] for t in range(6)] for i in range(depth)]

    y_prompt, (new_k, new_v, new_hgrn, new_gdn) = _trunk(
        x_prompt.reshape(batch * seq, d), mods_ctx, batch * seq, seq, P, None)

    cos, sin = _rope_tables(dec_seq)
    past = cache_attn_k.shape[2]
    cache = dict(attn_k=cache_attn_k.reshape(dec_batch, -1, past, d),
                 attn_v=cache_attn_v.reshape(dec_batch, -1, past, d),
                 hgrn=state_hgrn, gdn=state_gdn, cos=cos, sin=sin)
    y_sample, _ = _trunk(x_sample.reshape(dec_batch * dec_seq, d), mods_lat, dec_seq, dec_seq, P,
                         cache)

    kv_shape = (batch, seq, N_HEADS, HEAD_DIM)
    new_k = jnp.stack([t.reshape(kv_shape) for t in new_k], axis=1)
    new_v = jnp.stack([t.reshape(kv_shape) for t in new_v], axis=1)
    return (y_prompt.reshape(batch, seq, d), y_sample.reshape(dec_batch, dec_seq, d),
            new_k, new_v, new_hgrn, new_gdn)
```

```python
import functools
import math

import jax
import jax.numpy as jnp
import numpy as np
from jax import lax
from jax.experimental import pallas as pl
from jax.experimental.pallas import tpu as pltpu

F32 = jnp.float32
BF16 = jnp.bfloat16

D_MODEL = 1024
N_HEADS = 8
HEAD_DIM = 128
DA_HD = 64
GRID_W = 64
ROPE_BASE = 10000.0
D_FF = 2816
EPS = 1e-6
CHUNK = 128
N_LEVELS = 7
ROW_TILE = 1024
FF_TILE = 256
GDN_GROUP = 4
VMEM_LIMIT = 56 * 1024 * 1024


def _params(sem, vmem=VMEM_LIMIT):
    return pltpu.CompilerParams(dimension_semantics=sem, vmem_limit_bytes=vmem)


def _dot(a, b):
    return jnp.dot(a, b, preferred_element_type=F32)


def _dot_nt(a, b):
    return lax.dot_general(a, b, (((1,), (1,)), ((), ())), preferred_element_type=F32)


def _dot_tn(a, b):
    return lax.dot_general(a, b, (((0,), (0,)), ((), ())), preferred_element_type=F32)


def _silu(x):
    return x * jax.nn.sigmoid(x)


def _norm_mod(x, g, sc, sh):
    y = x * lax.rsqrt(jnp.mean(x * x, axis=-1, keepdims=True) + EPS)
    return (y * g) * (1.0 + sc) + sh


def _mod_kernel(c_ref, w_ref, b_ref, o_ref):
    s = _silu(c_ref[...]).astype(BF16)
    o_ref[...] = _dot(s, w_ref[...].astype(BF16)) + b_ref[...]


def modulation_all(cond, w_mod, b_mod):
    depth, d, n = w_mod.shape
    tn = 1024
    return pl.pallas_call(
        _mod_kernel,
        grid=(depth, n // tn),
        in_specs=[pl.BlockSpec((16, d), lambda l, j: (0, 0)),
                  pl.BlockSpec((None, d, tn), lambda l, j: (l, 0, j)),
                  pl.BlockSpec((None, 1, tn), lambda l, j: (l, 0, j))],
        out_specs=pl.BlockSpec((None, 16, tn), lambda l, j: (l, 0, j)),
        out_shape=jax.ShapeDtypeStruct((depth, 16, n), F32),
        compiler_params=_params(("arbitrary", "arbitrary")),
        name="modulation",
    )(cond, w_mod, b_mod.reshape(depth, 1, n))


def _proj_kernel(x_ref, g_ref, sc_ref, sh_ref, w_ref, o_ref, h_ref):
    @pl.when(pl.program_id(1) == 0)
    def _():
        h_ref[...] = _norm_mod(x_ref[...], g_ref[...], sc_ref[...], sh_ref[...]).astype(BF16)

    o_ref[...] = _dot(h_ref[...], w_ref[...])


def norm_proj(x, g, sc, sh, w, rows_per_mod, tn):
    r, d = x.shape
    n = w.shape[1]
    tm = ROW_TILE
    mod_spec = pl.BlockSpec((None, 1, d), lambda i, j: ((i * tm) // rows_per_mod, 0, 0))
    return pl.pallas_call(
        _proj_kernel,
        grid=(r // tm, n // tn),
        in_specs=[pl.BlockSpec((tm, d), lambda i, j: (i, 0)),
                  pl.BlockSpec((1, d), lambda i, j: (0, 0)),
                  mod_spec, mod_spec,
                  pl.BlockSpec((d, tn), lambda i, j: (0, j))],
        out_specs=pl.BlockSpec((tm, tn), lambda i, j: (i, j)),
        out_shape=jax.ShapeDtypeStruct((r, n), F32),
        scratch_shapes=[pltpu.VMEM((tm, d), BF16)],
        compiler_params=_params(("arbitrary", "arbitrary")),
        name="norm_proj",
    )(x, g.reshape(1, d), sc, sh, w)


def _out_proj_kernel(o_ref, w_ref, x_ref, gate_ref, y_ref):
    y_ref[...] = x_ref[...] + gate_ref[...] * _dot(o_ref[...], w_ref[...])


def out_proj_residual(o, w, x, gate, rows_per_mod):
    r, d = x.shape
    tm = ROW_TILE
    return pl.pallas_call(
        _out_proj_kernel,
        grid=(r // tm,),
        in_specs=[pl.BlockSpec((tm, d), lambda i: (i, 0)),
                  pl.BlockSpec((d, d), lambda i: (0, 0)),
                  pl.BlockSpec((tm, d), lambda i: (i, 0)),
                  pl.BlockSpec((None, 1, d), lambda i: ((i * tm) // rows_per_mod, 0, 0))],
        out_specs=pl.BlockSpec((tm, d), lambda i: (i, 0)),
        out_shape=jax.ShapeDtypeStruct((r, d), F32),
        compiler_params=_params(("arbitrary",)),
        name="out_proj",
    )(o, w, x, gate)


def _ffn_kernel(x_ref, g_ref, sc_ref, sh_ref, gate_ref, wv_ref, wg_ref, cv_ref, cg_ref, bv_ref,
                bg_ref, wd_ref, fg_ref, y_ref, h_ref, acc_ref, *, seq_len, final_norm):
    j = pl.program_id(1)

    @pl.when(j == 0)
    def _():
        h_ref[...] = _norm_mod(x_ref[...], g_ref[...], sc_ref[...], sh_ref[...]).astype(BF16)
        acc_ref[...] = jnp.zeros_like(acc_ref)

    tm = h_ref.shape[0]
    h = h_ref[...]
    pos = lax.broadcasted_iota(jnp.int32, (tm, FF_TILE), 0) & (seq_len - 1)
    first = pos == 0
    last = pos == seq_len - 1

    def conv(u, cw_ref, b_ref):
        prev = jnp.where(first, 0.0, pltpu.roll(u, 1, 0))
        nxt = jnp.where(last, 0.0, pltpu.roll(u, tm - 1, 0))
        return prev * cw_ref[0:1, :] + u * cw_ref[1:2, :] + nxt * cw_ref[2:3, :] + b_ref[...]

    val = conv(_dot(h, wv_ref[...]), cv_ref, bv_ref)
    gte = conv(_dot(h, wg_ref[...]), cg_ref, bg_ref)
    act = (val * _silu(gte)).astype(BF16)
    acc_ref[...] += _dot(act, wd_ref[...])

    @pl.when(j == pl.num_programs(1) - 1)
    def _():
        y = x_ref[...] + gate_ref[...] * acc_ref[...]
        if final_norm:
            y = y * lax.rsqrt(jnp.mean(y * y, axis=-1, keepdims=True) + EPS) * fg_ref[...]
        y_ref[...] = y


def conv_ffn_residual(x, g, sc, sh, gate, w_up, conv_w, conv_b, w_down, final_g, rows_per_mod,
                      seq_len, final_norm):
    r, d = x.shape
    tm = ROW_TILE
    nf = D_FF // FF_TILE
    mod_spec = pl.BlockSpec((None, 1, d), lambda i, j: ((i * tm) // rows_per_mod, 0, 0))
    kern = functools.partial(_ffn_kernel, seq_len=seq_len, final_norm=final_norm)
    conv_b = conv_b.reshape(1, 2 * D_FF)
    return pl.pallas_call(
        kern,
        grid=(r // tm, nf),
        in_specs=[pl.BlockSpec((tm, d), lambda i, j: (i, 0)),
                  pl.BlockSpec((1, d), lambda i, j: (0, 0)),
                  mod_spec, mod_spec, mod_spec,
                  pl.BlockSpec((d, FF_TILE), lambda i, j: (0, j)),
                  pl.BlockSpec((d, FF_TILE), lambda i, j: (0, nf + j)),
                  pl.BlockSpec((3, FF_TILE), lambda i, j: (0, j)),
                  pl.BlockSpec((3, FF_TILE), lambda i, j: (0, nf + j)),
                  pl.BlockSpec((1, FF_TILE), lambda i, j: (0, j)),
                  pl.BlockSpec((1, FF_TILE), lambda i, j: (0, nf + j)),
                  pl.BlockSpec((FF_TILE, d), lambda i, j: (j, 0)),
                  pl.BlockSpec((1, d), lambda i, j: (0, 0))],
        out_specs=pl.BlockSpec((tm, d), lambda i, j: (i, 0)),
        out_shape=jax.ShapeDtypeStruct((r, d), F32),
        scratch_shapes=[pltpu.VMEM((tm, d), BF16), pltpu.VMEM((tm, d), F32)],
        compiler_params=_params(("arbitrary", "arbitrary")),
        name="conv_ffn",
    )(x, g.reshape(1, d), sc, sh, gate, w_up, w_up, conv_w, conv_w, conv_b, conv_b, w_down,
      final_g.reshape(1, d))


def _diff_softmax_pv(q, k, v, lam):
    lane = lax.broadcasted_iota(jnp.int32, q.shape, 1)
    outs = []
    for c in range(2):
        qc = jnp.where((lane < DA_HD) == (c == 0), q, 0.0).astype(BF16)
        s = _dot_nt(qc, k)
        e = jnp.exp(s - jnp.max(s, axis=-1, keepdims=True))
        l = jnp.sum(e, axis=-1, keepdims=True)
        outs.append(_dot(e.astype(BF16), v) / l)
    return outs[0] - lam * outs[1]


def _subln(o, w, post_scale):
    return o * lax.rsqrt(jnp.mean(o * o, axis=-1, keepdims=True) + 1e-5) * w * post_scale


def _attn_ctx_kernel(lam_ref, q_ref, k_ref, v_ref, w_ref, o_ref, *, post_scale):
    lam = lam_ref[0]
    for h in range(N_HEADS):
        sl = slice(h * HEAD_DIM, (h + 1) * HEAD_DIM)
        q = q_ref[:, sl] * (DA_HD ** -0.5)
        o = _diff_softmax_pv(q, k_ref[:, sl].astype(BF16), v_ref[:, sl].astype(BF16), lam)
        o_ref[:, sl] = _subln(o, w_ref[...], post_scale).astype(BF16)


def attn_context(qkv, lam, subln_w, seq_len, post_scale):
    r = qkv.shape[0]
    d = D_MODEL
    kern = functools.partial(_attn_ctx_kernel, post_scale=post_scale)
    return pl.pallas_call(
        kern,
        grid=(r // seq_len,),
        in_specs=[pl.BlockSpec(memory_space=pltpu.SMEM),
                  pl.BlockSpec((seq_len, d), lambda b: (b, 0)),
                  pl.BlockSpec((seq_len, d), lambda b: (b, 1)),
                  pl.BlockSpec((seq_len, d), lambda b: (b, 2)),
                  pl.BlockSpec((1, HEAD_DIM), lambda b: (0, 0))],
        out_specs=pl.BlockSpec((seq_len, d), lambda b: (b, 0)),
        out_shape=jax.ShapeDtypeStruct((r, d), BF16),
        compiler_params=_params(("arbitrary",)),
        name="attn_context",
    )(lam, qkv, qkv, qkv, subln_w.reshape(1, HEAD_DIM))


def _rope(x, cos, sin):
    lane = lax.broadcasted_iota(jnp.int32, x.shape, 1)
    partner = jnp.where((lane & 16) == 0, pltpu.roll(x, HEAD_DIM - 16, 1), pltpu.roll(x, 16, 1))
    return x * cos + partner * sin


def _attn_lat_kernel(lam_ref, q_ref, k_ref, v_ref, ck_ref, cv_ref, cos_ref, sin_ref, w_ref, o_ref,
                     kcat_ref, vcat_ref, *, post_scale, q_tile):
    lam = lam_ref[0]
    past = ck_ref.shape[0]
    n = q_ref.shape[0]
    kcat_ref[0:past, :] = ck_ref[...].astype(BF16)
    vcat_ref[0:past, :] = cv_ref[...].astype(BF16)
    kcat_ref[past:past + n, :] = _rope(k_ref[...], cos_ref[...], sin_ref[...]).astype(BF16)
    vcat_ref[past:past + n, :] = v_ref[...].astype(BF16)
    k = kcat_ref[...]
    v = vcat_ref[...]
    for t in range(n // q_tile):
        rows = slice(t * q_tile, (t + 1) * q_tile)
        q = _rope(q_ref[rows, :], cos_ref[rows, :], sin_ref[rows, :]) * (DA_HD ** -0.5)
        o = _diff_softmax_pv(q, k, v, lam)
        o_ref[rows, :] = _subln(o, w_ref[...], post_scale).astype(BF16)


def attn_latent(qkv, cache_k, cache_v, layer_idx, cos, sin, lam, subln_w, seq_len, post_scale):
    r = qkv.shape[0]
    b = r // seq_len
    past = cache_k.shape[2]
    kern = functools.partial(_attn_lat_kernel, post_scale=post_scale, q_tile=256)
    blk = (seq_len, HEAD_DIM)
    cache_spec = pl.BlockSpec((None, None, past, HEAD_DIM), lambda i, h: (i, layer_idx, 0, h))
    tab_spec = pl.BlockSpec((seq_len, HEAD_DIM), lambda i, h: (0, 0))
    return pl.pallas_call(
        kern,
        grid=(b, N_HEADS),
        in_specs=[pl.BlockSpec(memory_space=pltpu.SMEM),
                  pl.BlockSpec(blk, lambda i, h: (i, h)),
                  pl.BlockSpec(blk, lambda i, h: (i, N_HEADS + h)),
                  pl.BlockSpec(blk, lambda i, h: (i, 2 * N_HEADS + h)),
                  cache_spec, cache_spec, tab_spec, tab_spec,
                  pl.BlockSpec((1, HEAD_DIM), lambda i, h: (0, 0))],
        out_specs=pl.BlockSpec(blk, lambda i, h: (i, h)),
        out_shape=jax.ShapeDtypeStruct((r, D_MODEL), BF16),
        scratch_shapes=[pltpu.VMEM((past + seq_len, HEAD_DIM), BF16),
                        pltpu.VMEM((past + seq_len, HEAD_DIM), BF16)],
        compiler_params=_params(("arbitrary", "arbitrary")),
        name="attn_latent",
    )(lam, qkv, qkv, qkv, cache_k, cache_v, cos, sin, subln_w.reshape(1, HEAD_DIM))


def _prefix_rows(x):
    row = lax.broadcasted_iota(jnp.int32, x.shape, 0)
    for j in range(N_LEVELS):
        s = 1 << j
        x = x + jnp.where(row >= s, pltpu.roll(x, s, 0), 0.0)
    return x


def _block_boundary(x, m):
    n, lanes = x.shape
    w = 2 * m
    if w >= 8:
        y = x.reshape(n // w, w, lanes)
        return jnp.broadcast_to(y[:, m - 1:m, :], y.shape).reshape(n, lanes)
    y = x.reshape(n // 8, 8, lanes)
    sub = lax.broadcasted_iota(jnp.int32, y.shape, 1)
    out = None
    for grp in range(8 // w):
        src = grp * w + m - 1
        b = jnp.broadcast_to(y[:, src:src + 1, :], y.shape)
        out = b if out is None else jnp.where(sub >= grp * w, b, out)
    return out.reshape(n, lanes)


def _level_masks():
    t = np.arange(CHUNK)
    ms = [(t[:, None] == t[None, :])]
    for j in range(1, N_LEVELS + 1):
        ms.append((t[:, None] >> j) == (t[None, :] >> j))
    return jnp.asarray(np.stack(ms).astype(np.float32))


def _gate_norm_out(o, gate, w):
    y = o * lax.rsqrt(jnp.mean(o * o, axis=-1, keepdims=True) + EPS) * w
    return (y * _silu(gate)).astype(BF16)


def _hgrn_chunk(q, z, v, lb, masks_ref, st, rev):
    e = jnp.exp(-jnp.abs(z))
    r = 1.0 / (1.0 + e)
    pos = z >= 0
    sig = jnp.where(pos, r, e * r)
    nsig = jnp.where(pos, e * r, r)
    f = lb + (1.0 - lb) * sig
    key = (1.0 - lb) * nsig
    lf = jnp.log(f)
    incl = _prefix_rows(lf)
    base = incl - lf if rev else incl
    row = lax.broadcasted_iota(jnp.int32, q.shape, 0)
    vb = v.astype(BF16)

    a = masks_ref[0] * _dot_nt(q.astype(BF16), key.astype(BF16))
    for lvl in range(N_LEVELS):
        m = 1 << lvl
        dm = base - _block_boundary(incl, m)
        upper = (row & m) != 0
        em = jnp.exp(jnp.where(upper, dm, -dm))
        qe = q * em
        ke = key * em
        if rev:
            qt = jnp.where(upper, 0.0, qe)
            kt = jnp.where(upper, ke, 0.0)
        else:
            qt = jnp.where(upper, qe, 0.0)
            kt = jnp.where(upper, 0.0, ke)
        a = a + masks_ref[lvl + 1] * _dot_nt(qt.astype(BF16), kt.astype(BF16))

    tot = incl[CHUNK - 1:CHUNK, :]
    if rev:
        e_in = jnp.exp(tot - base)
        e_out = jnp.exp(base)
    else:
        e_in = jnp.exp(incl)
        e_out = jnp.exp(tot - incl)
    o = _dot(a.astype(BF16), vb) + _dot_nt((q * e_in).astype(BF16), st.astype(BF16))
    st_new = jnp.exp(tot) * st + _dot_tn(vb, (key * e_out).astype(BF16))
    return o, st_new


def _hgrn_kernel(*refs, has_init, emit_state):
    refs = list(refs)
    q_ref, zf_ref, zb_ref, i_ref, g_ref, lb_ref, w_ref, masks_ref = refs[:8]
    refs = refs[8:]
    s0_ref = refs.pop(0) if has_init else None
    o_ref = refs.pop(0)
    s_out_ref = refs.pop(0) if emit_state else None
    of_ref, ob_ref, st_ref = refs

    n = q_ref.shape[0]
    nc = n // CHUNK
    lb = lb_ref[...]
    for d in range(2):
        st_ref[d] = s0_ref[d].T if has_init else jnp.zeros((HEAD_DIM, HEAD_DIM), F32)

    def body(c, carry):
        rf = pl.ds(pl.multiple_of(c * CHUNK, CHUNK), CHUNK)
        rb = pl.ds(pl.multiple_of((nc - 1 - c) * CHUNK, CHUNK), CHUNK)
        scale = HEAD_DIM ** -0.5
        o, s = _hgrn_chunk(q_ref[rf, :] * scale, zf_ref[rf, :], i_ref[rf, :], lb, masks_ref,
                           st_ref[0], False)
        of_ref[rf, :] = o
        st_ref[0] = s
        o, s = _hgrn_chunk(q_ref[rb, :] * scale, zb_ref[rb, :], i_ref[rb, :], lb, masks_ref,
                           st_ref[1], True)
        ob_ref[rb, :] = o
        st_ref[1] = s
        return carry

    lax.fori_loop(0, nc, body, 0)
    o_ref[...] = _gate_norm_out(of_ref[...] + ob_ref[...], g_ref[...], w_ref[...])
    if emit_state:
        for d in range(2):
            s_out_ref[d] = st_ref[d].T


def hgrn_scan(proj, lb, norm_w, masks, seq_len, state_in, layer_idx, emit_state):
    r = proj.shape[0]
    b = r // seq_len
    has_init = state_in is not None
    blk = (seq_len, HEAD_DIM)
    col = lambda k: pl.BlockSpec(blk, lambda i, h, k=k: (i, k * N_HEADS + h))
    state_spec = pl.BlockSpec((None, None, 2, None, HEAD_DIM, HEAD_DIM),
                              lambda i, h: (i, layer_idx, 0, h, 0, 0))
    in_specs = [col(0), col(1), col(2), col(3), col(4),
                pl.BlockSpec((None, 1, HEAD_DIM), lambda i, h: (h, 0, 0)),
                pl.BlockSpec((1, HEAD_DIM), lambda i, h: (0, 0)),
                pl.BlockSpec((N_LEVELS + 1, CHUNK, CHUNK), lambda i, h: (0, 0, 0))]
    args = [proj, proj, proj, proj, proj, lb.reshape(N_HEADS, 1, HEAD_DIM),
            norm_w.reshape(1, HEAD_DIM), masks]
    if has_init:
        in_specs.append(state_spec)
        args.append(state_in)
    out_specs = [pl.BlockSpec(blk, lambda i, h: (i, h))]
    out_shape = [jax.ShapeDtypeStruct((r, D_MODEL), BF16)]
    if emit_state:
        out_specs.append(pl.BlockSpec((None, None, 2, None, HEAD_DIM, HEAD_DIM),
                                      lambda i, h: (i, 0, 0, h, 0, 0)))
        out_shape.append(jax.ShapeDtypeStruct((b, 1, 2, N_HEADS, HEAD_DIM, HEAD_DIM), F32))
    kern = functools.partial(_hgrn_kernel, has_init=has_init, emit_state=emit_state)
    res = pl.pallas_call(
        kern,
        grid=(b, N_HEADS),
        in_specs=in_specs,
        out_specs=out_specs,
        out_shape=out_shape,
        scratch_shapes=[pltpu.VMEM(blk, F32), pltpu.VMEM(blk, F32),
                        pltpu.VMEM((2, HEAD_DIM, HEAD_DIM), F32)],
        compiler_params=_params(("arbitrary", "arbitrary")),
        name="hgrn_scan",
    )(*args)
    return res if emit_state else (res[0], None)


def _lane_column(x, lane_idx):
    lane = lax.broadcasted_iota(jnp.int32, x.shape, 1)
    colv = jnp.sum(jnp.where(lane == lane_idx, x, 0.0), axis=1, keepdims=True)
    return jnp.broadcast_to(colv, x.shape)


def _gdn_prepare(items, masks_ref):
    row = lax.broadcasted_iota(jnp.int32, (CHUNK, CHUNK), 0)
    colm = lax.broadcasted_iota(jnp.int32, (CHUNK, CHUNK), 1)
    pre = []
    for q, k, v, log_a, beta, kk, qk, rev in items:
        incl = _prefix_rows(log_a)
        tot = incl[CHUNK - 1:CHUNK, :]
        g = tot - incl + log_a if rev else incl
        g_end = g[0:1, :] if rev else tot
        within = (row <= colm) if rev else (row >= colm)
        strict = (row < colm) if rev else (row > colm)
        decay = jnp.exp(jnp.where(within, g - g.T, -jnp.inf))
        a = jnp.where(strict, beta * kk * decay, 0.0)
        eg = jnp.exp(g)
        x = jnp.concatenate([v * beta, k * (beta * eg)], axis=1).astype(BF16)
        p = (qk * decay).astype(BF16)
        ke = (k * jnp.exp(g_end - g)).astype(BF16)
        pre.append((a, x, p, ke, q * eg, jnp.exp(g_end)))
    ts = [masks_ref[0] - it[0] * (masks_ref[1] - masks_ref[0]) for it in pre]
    for lvl in range(1, N_LEVELS):
        sel = masks_ref[lvl + 1] - masks_ref[lvl]
        tbs = [t.astype(BF16) for t in ts]
        lts = [_dot((it[0] * sel).astype(BF16), tb).astype(BF16) for it, tb in zip(pre, tbs)]
        ts = [t - _dot(tb, lt) for t, tb, lt in zip(ts, tbs, lts)]
    wus = [_dot(t.astype(BF16), it[1]).astype(BF16) for t, it in zip(ts, pre)]
    pwus = [_dot(it[2], wu) for it, wu in zip(pre, wus)]
    kwus = [_dot_tn(it[3], wu) for it, wu in zip(pre, wus)]
    out = []
    for it, pwu, kwu in zip(pre, pwus, kwus):
        qs = (it[4] - pwu[:, HEAD_DIM:]).astype(BF16)
        out.append((qs, pwu[:, :HEAD_DIM], kwu[:, HEAD_DIM:].astype(BF16), kwu[:, :HEAD_DIM], it[5]))
    return out


def _gdn_kernel(*refs, has_init, emit_state):
    refs = list(refs)
    (q_ref, k_ref, v_ref, gate_ref, ab_ref, cq_ref, ck_ref, cv_ref, rate_ref, bias_ref,
     w_ref, masks_ref) = refs[:12]
    refs = refs[12:]
    s0_ref = refs.pop(0) if has_init else None
    o_ref = refs.pop(0)
    s_out_ref = refs.pop(0) if emit_state else None
    qn_ref, kn_ref, vn_ref, qs_ref, ms_ref, o_acc_ref, b_ref, dec_ref, st_ref = refs

    n = q_ref.shape[0]
    nc = n // CHUNK
    group = min(nc, GDN_GROUP)
    h = pl.program_id(1)
    pos = lax.broadcasted_iota(jnp.int32, (n, HEAD_DIM), 0)

    def conv_silu(x_ref, cw_ref):
        x = x_ref[...]
        prev = jnp.where(pos == 0, 0.0, pltpu.roll(x, 1, 0))
        nxt = jnp.where(pos == n - 1, 0.0, pltpu.roll(x, n - 1, 0))
        return _silu(prev * cw_ref[0:1, :] + x * cw_ref[1:2, :] + nxt * cw_ref[2:3, :])

    def l2norm(x):
        return x * lax.rsqrt(jnp.sum(x * x, axis=-1, keepdims=True) + 1e-6)

    qn_ref[...] = l2norm(conv_silu(q_ref, cq_ref)) * (HEAD_DIM ** -0.5)
    kn_ref[...] = l2norm(conv_silu(k_ref, ck_ref))
    vn_ref[...] = conv_silu(v_ref, cv_ref)
    for d in range(2):
        st_ref[d] = s0_ref[d] if has_init else jnp.zeros((HEAD_DIM, HEAD_DIM), F32)

    def chunk_rows(c):
        return pl.ds(pl.multiple_of(c * CHUNK, CHUNK), CHUNK)

    def prepare_body(i, carry):
        items, dest = [], []
        for grp in range(group):
            rows = chunk_rows(i * group + grp)
            q = qn_ref[rows, :]
            k = kn_ref[rows, :]
            v = vn_ref[rows, :]
            ab = ab_ref[rows, :]
            la_all = -rate_ref[...] * jax.nn.softplus(ab + bias_ref[...])
            sg_all = jax.nn.sigmoid(ab)
            kb = k.astype(BF16)
            kk = _dot_nt(kb, kb)
            qk = _dot_nt(q.astype(BF16), kb)
            for d in range(2):
                log_a = _lane_column(la_all, d * N_HEADS + h)
                beta = _lane_column(sg_all, (2 + d) * N_HEADS + h)
                items.append((q, k, v, log_a, beta, kk, qk, d == 1))
                dest.append((d, rows))
        for (d, rows), (qs, o0, ms, b, dec) in zip(dest, _gdn_prepare(items, masks_ref)):
            qs_ref[d, rows, :] = qs
            ms_ref[d, rows, :] = ms
            o_acc_ref[d, rows, :] = o0
            b_ref[d, rows, :] = b
            dec_ref[d, rows, :] = jnp.broadcast_to(dec, (CHUNK, HEAD_DIM))
        return carry

    lax.fori_loop(0, nc // group, prepare_body, 0)

    def scan_body(c, carry):
        for d, cc in ((0, c), (1, nc - 1 - c)):
            rows = chunk_rows(cc)
            s = st_ref[d]
            sb = s.astype(BF16)
            o_acc_ref[d, rows, :] += _dot(qs_ref[d, rows, :], sb)
            st_ref[d] = dec_ref[d, rows, :] * s - _dot(ms_ref[d, rows, :], sb) + b_ref[d, rows, :]
        return carry

    lax.fori_loop(0, nc, scan_body, 0)
    o_ref[...] = _gate_norm_out(o_acc_ref[0] + o_acc_ref[1], gate_ref[...], w_ref[...])
    if emit_state:
        for d in range(2):
            s_out_ref[d] = st_ref[d]


def gdn_scan(proj, ab, conv_w, rate, bias, norm_w, masks, seq_len, state_in, layer_idx,
             emit_state):
    r = proj.shape[0]
    b = r // seq_len
    has_init = state_in is not None
    blk = (seq_len, HEAD_DIM)
    col = lambda k: pl.BlockSpec(blk, lambda i, h, k=k: (i, k * N_HEADS + h))
    ccol = lambda k: pl.BlockSpec((3, HEAD_DIM), lambda i, h, k=k: (0, k * N_HEADS + h))
    vec = pl.BlockSpec((1, HEAD_DIM), lambda i, h: (0, 0))
    state_spec = pl.BlockSpec((None, None, 2, None, HEAD_DIM, HEAD_DIM),
                              lambda i, h: (i, layer_idx, 0, h, 0, 0))
    in_specs = [col(0), col(1), col(2), col(3), pl.BlockSpec(blk, lambda i, h: (i, 0)),
                ccol(0), ccol(1), ccol(2), vec, vec, vec,
                pl.BlockSpec((N_LEVELS + 1, CHUNK, CHUNK), lambda i, h: (0, 0, 0))]
    args = [proj, proj, proj, proj, ab, conv_w, conv_w, conv_w, rate, bias,
            norm_w.reshape(1, HEAD_DIM), masks]
    if has_init:
        in_specs.append(state_spec)
        args.append(state_in)
    out_specs = [pl.BlockSpec(blk, lambda i, h: (i, h))]
    out_shape = [jax.ShapeDtypeStruct((r, D_MODEL), BF16)]
    if emit_state:
        out_specs.append(pl.BlockSpec((None, None, 2, None, HEAD_DIM, HEAD_DIM),
                                      lambda i, h: (i, 0, 0, h, 0, 0)))
        out_shape.append(jax.ShapeDtypeStruct((b, 1, 2, N_HEADS, HEAD_DIM, HEAD_DIM), F32))
    kern = functools.partial(_gdn_kernel, has_init=has_init, emit_state=emit_state)
    res = pl.pallas_call(
        kern,
        grid=(b, N_HEADS),
        in_specs=in_specs,
        out_specs=out_specs,
        out_shape=out_shape,
        scratch_shapes=[pltpu.VMEM(blk, F32)] * 3
        + [pltpu.VMEM((2,) + blk, BF16)] * 2 + [pltpu.VMEM((2,) + blk, F32)] * 3
        + [pltpu.VMEM((2, HEAD_DIM, HEAD_DIM), F32)],
        compiler_params=_params(("arbitrary", "arbitrary")),
        name="gdn_scan",
    )(*args)
    return res if emit_state else (res[0], None)


def _rope_tables(n):
    rows = n // GRID_W
    row = jnp.repeat(jnp.arange(rows, dtype=F32), GRID_W)
    colp = jnp.tile(jnp.arange(GRID_W, dtype=F32), rows)
    nf = DA_HD // 4
    inv = ROPE_BASE ** (-jnp.arange(nf, dtype=F32) / nf)
    ar = row[:, None] * inv
    ac = colp[:, None] * inv
    cr, sr, cc, sc = jnp.cos(ar), jnp.sin(ar), jnp.cos(ac), jnp.sin(ac)
    cos = jnp.concatenate([cr, cr, cc, cc] * 2, axis=1)
    sin = jnp.concatenate([-sr, sr, -sc, sc] * 2, axis=1)
    return cos, sin


def _trunk(x, mods, rows_per_mod, seq_len, P, cache):
    ctx_mode = cache is None
    depth = P['w_mod'].shape[0]
    new_k, new_v, hgrn_state, gdn_state = [], [], None, None
    for i in range(depth):
        sh1, sc1, g1, sh2, sc2, g2 = mods[i]
        kind, j = i % 3, i // 3
        if kind == 0:
            qkv = norm_proj(x, P['norm_g'][i, 0], sc1, sh1, P['attn_w_in'][j], rows_per_mod, 1024)
            lam_init = 0.8 - 0.6 * math.exp(-0.3 * i)
            lp = P['attn_lambda'][j]
            lam = (jnp.exp(jnp.sum(lp[0] * lp[1])) - jnp.exp(jnp.sum(lp[2] * lp[3]))
                   + lam_init).reshape(1)
            if ctx_mode:
                o = attn_context(qkv, lam, P['attn_subln'][j], seq_len, 1.0 - lam_init)
                new_k.append(qkv[:, D_MODEL:2 * D_MODEL])
                new_v.append(qkv[:, 2 * D_MODEL:])
            else:
                o = attn_latent(qkv, cache['attn_k'], cache['attn_v'], j, cache['cos'],
                                cache['sin'], lam, P['attn_subln'][j], seq_len, 1.0 - lam_init)
            w_out = P['attn_w_out'][j]
        elif kind == 1:
            proj = norm_proj(x, P['norm_g'][i, 0], sc1, sh1, P['hgrn_w_in'][j], rows_per_mod, 1024)
            o, st = hgrn_scan(proj, P['lb_all'][i], P['hgrn_norm'][j], P['masks'], seq_len,
                              None if ctx_mode else cache['hgrn'], j, ctx_mode)
            if ctx_mode:
                hgrn_state = st
            w_out = P['hgrn_w_out'][j]
        else:
            proj = norm_proj(x, P['norm_g'][i, 0], sc1, sh1, P['gdn_w_main'][j], rows_per_mod, 1024)
            ab = norm_proj(x, P['norm_g'][i, 0], sc1, sh1, P['gdn_w_tail'][j], rows_per_mod,
                           HEAD_DIM)
            o, st = gdn_scan(proj, ab, P['gdn_conv'][j], P['gdn_rate'][j], P['gdn_bias'][j],
                             P['gdn_norm'][j], P['masks'], seq_len,
                             None if ctx_mode else cache['gdn'], j, ctx_mode)
            if ctx_mode:
                gdn_state = st
            w_out = P['gdn_w_out'][j]
        x = out_proj_residual(o, w_out, x, g1, rows_per_mod)
        x = conv_ffn_residual(x, P['norm_g'][i, 1], sc2, sh2, g2, P['ffn_w_up'][i],
                              P['ffn_conv'][i], P['ffn_conv_b'][i], P['ffn_w_down'][i],
                              P['final_g'], rows_per_mod, seq_len, i == depth - 1)
    return x, (new_k, new_v, hgrn_state, gdn_state)


def kernel(x_prompt, x_sample, cache_attn_k, cache_attn_v, state_hgrn, state_gdn, c, c_ctx, norm_g, w_mod, b_mod, final_g, attn_w_in, attn_lambda, attn_subln, attn_w_out, hgrn_w_in, hgrn_lb, hgrn_norm, hgrn_w_out, gdn_w_in, gdn_conv, gdn_a_log, gdn_dt_bias, gdn_norm, gdn_w_out, ffn_w_up, ffn_conv, ffn_conv_b, ffn_w_down):
    batch, seq, d = x_prompt.shape
    dec_batch, dec_seq, _ = x_sample.shape
    depth = w_mod.shape[0]
    n_gdn = gdn_w_in.shape[0]

    lb_all = jnp.cumsum(jax.nn.softmax(hgrn_lb, axis=0), axis=0)
    lb_all = lb_all - lb_all[0]

    pad = jnp.zeros((n_gdn, HEAD_DIM - 2 * N_HEADS), F32)
    gdn_rate = jnp.concatenate([jnp.exp(gdn_a_log).reshape(n_gdn, 2 * N_HEADS), pad], axis=1)
    gdn_bias = jnp.concatenate([gdn_dt_bias.reshape(n_gdn, 2 * N_HEADS), pad], axis=1)
    tail = gdn_w_in[:, :, 4 * d:]
    tail = jnp.concatenate([tail, jnp.zeros((n_gdn, d, HEAD_DIM - tail.shape[2]), F32)], axis=2)

    P = dict(norm_g=norm_g, w_mod=w_mod, final_g=final_g,
             attn_w_in=attn_w_in.astype(BF16), attn_lambda=attn_lambda, attn_subln=attn_subln,
             attn_w_out=attn_w_out.astype(BF16),
             hgrn_w_in=hgrn_w_in.astype(BF16), lb_all=lb_all, hgrn_norm=hgrn_norm,
             hgrn_w_out=hgrn_w_out.astype(BF16),
             gdn_w_main=gdn_w_in[:, :, :4 * d].astype(BF16), gdn_w_tail=tail.astype(BF16),
             gdn_conv=gdn_conv, gdn_rate=gdn_rate.reshape(n_gdn, 1, HEAD_DIM),
             gdn_bias=gdn_bias.reshape(n_gdn, 1, HEAD_DIM), gdn_norm=gdn_norm,
             gdn_w_out=gdn_w_out.astype(BF16),
             ffn_w_up=ffn_w_up.astype(BF16), ffn_conv=ffn_conv, ffn_conv_b=ffn_conv_b,
             ffn_w_down=ffn_w_down.astype(BF16), masks=_level_masks())

    cond = jnp.zeros((16, d), F32).at[:dec_batch].set(c).at[dec_batch].set(c_ctx)
    mod = modulation_all(cond, w_mod, b_mod).reshape(depth, 16, 6, 1, d)
    mods_lat = [[mod[i, :dec_batch, t] for t in range(6)] for i in range(depth)]
    mods_ctx = [[mod[i, dec_batch:dec_batch + 1, t] for t in range(6)] for i in range(depth)]

    y_prompt, (new_k, new_v, new_hgrn, new_gdn) = _trunk(
        x_prompt.reshape(batch * seq, d), mods_ctx, batch * seq, seq, P, None)

    cos, sin = _rope_tables(dec_seq)
    past = cache_attn_k.shape[2]
    cache = dict(attn_k=cache_attn_k.reshape(dec_batch, -1, past, d),
                 attn_v=cache_attn_v.reshape(dec_batch, -1, past, d),
                 hgrn=state_hgrn, gdn=state_gdn, cos=cos, sin=sin)
    y_sample, _ = _trunk(x_sample.reshape(dec_batch * dec_seq, d), mods_lat, dec_seq, dec_seq, P,
                         cache)

    kv_shape = (batch, seq, N_HEADS, HEAD_DIM)
    new_k = jnp.stack([t.reshape(kv_shape) for t in new_k], axis=1)
    new_v = jnp.stack([t.reshape(kv_shape) for t in new_v], axis=1)
    return (y_prompt.reshape(batch, seq, d), y_sample.reshape(dec_batch, dec_seq, d),
            new_k, new_v, new_hgrn, new_gdn)
```

```python
import functools
import math

import jax
import jax.numpy as jnp
import numpy as np
from jax import lax
from jax.experimental import pallas as pl
from jax.experimental.pallas import tpu as pltpu

F32 = jnp.float32
BF16 = jnp.bfloat16

D_MODEL = 1024
N_HEADS = 8
HEAD_DIM = 128
DA_HD = 64
GRID_W = 64
ROPE_BASE = 10000.0
D_FF = 2816
EPS = 1e-6
LOG2E = 1.4426950408889634
CHUNK = 128
N_LEVELS = 7
ROW_TILE = 1024
FF_TILE = 256
Q_TILE = 256
GDN_CHAINS = 8
VMEM_LIMIT = 56 * 1024 * 1024


def _params(sem, vmem=VMEM_LIMIT):
    return pltpu.CompilerParams(dimension_semantics=sem, vmem_limit_bytes=vmem)


def _dot(a, b):
    return jnp.dot(a, b, preferred_element_type=F32)


def _dot_nt(a, b):
    return lax.dot_general(a, b, (((1,), (1,)), ((), ())), preferred_element_type=F32)


def _dot_tn(a, b):
    return lax.dot_general(a, b, (((0,), (0,)), ((), ())), preferred_element_type=F32)


def _silu(x):
    return x * jax.nn.sigmoid(x)


def _norm_mod(x, g, sc, sh):
    y = x * lax.rsqrt(jnp.mean(x * x, axis=-1, keepdims=True) + EPS)
    return (y * g) * (1.0 + sc) + sh


def _mod_kernel(c_ref, w_ref, b_ref, o_ref):
    s = _silu(c_ref[...]).astype(BF16)
    o_ref[...] = _dot(s, w_ref[...].astype(BF16)) + b_ref[...]


def modulation_all(cond, w_mod, b_mod):
    depth, d, n = w_mod.shape
    tn = 1024
    return pl.pallas_call(
        _mod_kernel,
        grid=(depth, n // tn),
        in_specs=[pl.BlockSpec((16, d), lambda l, j: (0, 0)),
                  pl.BlockSpec((None, d, tn), lambda l, j: (l, 0, j)),
                  pl.BlockSpec((None, 1, tn), lambda l, j: (l, 0, j))],
        out_specs=pl.BlockSpec((None, 16, tn), lambda l, j: (l, 0, j)),
        out_shape=jax.ShapeDtypeStruct((depth, 16, n), F32),
        compiler_params=_params(("arbitrary", "arbitrary")),
        name="modulation",
    )(cond, w_mod, b_mod.reshape(depth, 1, n))


def _proj_kernel(x_ref, g_ref, sc_ref, sh_ref, w_ref, o_ref, h_ref):
    @pl.when(pl.program_id(1) == 0)
    def _():
        h_ref[...] = _norm_mod(x_ref[...], g_ref[...], sc_ref[...], sh_ref[...]).astype(BF16)

    o_ref[...] = _dot(h_ref[...], w_ref[...]).astype(o_ref.dtype)


def norm_proj(x, g, sc, sh, w, rows_per_mod, tn, out_dtype):
    r, d = x.shape
    n = w.shape[1]
    tm = ROW_TILE
    mod_spec = pl.BlockSpec((None, 1, d), lambda i, j: ((i * tm) // rows_per_mod, 0, 0))
    return pl.pallas_call(
        _proj_kernel,
        grid=(r // tm, n // tn),
        in_specs=[pl.BlockSpec((tm, d), lambda i, j: (i, 0)),
                  pl.BlockSpec((1, d), lambda i, j: (0, 0)),
                  mod_spec, mod_spec,
                  pl.BlockSpec((d, tn), lambda i, j: (0, j))],
        out_specs=pl.BlockSpec((tm, tn), lambda i, j: (i, j)),
        out_shape=jax.ShapeDtypeStruct((r, n), out_dtype),
        scratch_shapes=[pltpu.VMEM((tm, d), BF16)],
        compiler_params=_params(("arbitrary", "arbitrary")),
        name="norm_proj",
    )(x, g.reshape(1, d), sc, sh, w)


def _out_proj_kernel(o_ref, w_ref, x_ref, gate_ref, y_ref):
    y_ref[...] = x_ref[...] + gate_ref[...] * _dot(o_ref[...], w_ref[...])


def out_proj_residual(o, w, x, gate, rows_per_mod):
    r, d = x.shape
    tm = ROW_TILE
    return pl.pallas_call(
        _out_proj_kernel,
        grid=(r // tm,),
        in_specs=[pl.BlockSpec((tm, d), lambda i: (i, 0)),
                  pl.BlockSpec((d, d), lambda i: (0, 0)),
                  pl.BlockSpec((tm, d), lambda i: (i, 0)),
                  pl.BlockSpec((None, 1, d), lambda i: ((i * tm) // rows_per_mod, 0, 0))],
        out_specs=pl.BlockSpec((tm, d), lambda i: (i, 0)),
        out_shape=jax.ShapeDtypeStruct((r, d), F32),
        compiler_params=_params(("arbitrary",)),
        name="out_proj",
    )(o, w, x, gate)


def _ffn_kernel(x_ref, g_ref, sc_ref, sh_ref, gate_ref, wv_ref, wg_ref, cv_ref, cg_ref, bv_ref,
                bg_ref, wd_ref, fg_ref, y_ref, h_ref, acc_ref, *, seq_len, final_norm):
    j = pl.program_id(1)

    @pl.when(j == 0)
    def _():
        h_ref[...] = _norm_mod(x_ref[...], g_ref[...], sc_ref[...], sh_ref[...]).astype(BF16)
        acc_ref[...] = jnp.zeros_like(acc_ref)

    tm = h_ref.shape[0]
    h = h_ref[...]
    pos = lax.broadcasted_iota(jnp.int32, (tm, FF_TILE), 0) & (seq_len - 1)
    first = pos == 0
    last = pos == seq_len - 1

    def conv(u, cw_ref, b_ref):
        prev = jnp.where(first, 0.0, pltpu.roll(u, 1, 0))
        nxt = jnp.where(last, 0.0, pltpu.roll(u, tm - 1, 0))
        return prev * cw_ref[0:1, :] + u * cw_ref[1:2, :] + nxt * cw_ref[2:3, :] + b_ref[...]

    val = conv(_dot(h, wv_ref[...]), cv_ref, bv_ref)
    gte = conv(_dot(h, wg_ref[...]), cg_ref, bg_ref)
    act = (val * _silu(gte)).astype(BF16)
    acc_ref[...] += _dot(act, wd_ref[...])

    @pl.when(j == pl.num_programs(1) - 1)
    def _():
        y = x_ref[...] + gate_ref[...] * acc_ref[...]
        if final_norm:
            y = y * lax.rsqrt(jnp.mean(y * y, axis=-1, keepdims=True) + EPS) * fg_ref[...]
        y_ref[...] = y


def conv_ffn_residual(x, g, sc, sh, gate, w_up, conv_w, conv_b, w_down, final_g, rows_per_mod,
                      seq_len, final_norm):
    r, d = x.shape
    tm = ROW_TILE
    nf = D_FF // FF_TILE
    mod_spec = pl.BlockSpec((None, 1, d), lambda i, j: ((i * tm) // rows_per_mod, 0, 0))
    kern = functools.partial(_ffn_kernel, seq_len=seq_len, final_norm=final_norm)
    conv_b = conv_b.reshape(1, 2 * D_FF)
    return pl.pallas_call(
        kern,
        grid=(r // tm, nf),
        in_specs=[pl.BlockSpec((tm, d), lambda i, j: (i, 0)),
                  pl.BlockSpec((1, d), lambda i, j: (0, 0)),
                  mod_spec, mod_spec, mod_spec,
                  pl.BlockSpec((d, FF_TILE), lambda i, j: (0, j)),
                  pl.BlockSpec((d, FF_TILE), lambda i, j: (0, nf + j)),
                  pl.BlockSpec((3, FF_TILE), lambda i, j: (0, j)),
                  pl.BlockSpec((3, FF_TILE), lambda i, j: (0, nf + j)),
                  pl.BlockSpec((1, FF_TILE), lambda i, j: (0, j)),
                  pl.BlockSpec((1, FF_TILE), lambda i, j: (0, nf + j)),
                  pl.BlockSpec((FF_TILE, d), lambda i, j: (j, 0)),
                  pl.BlockSpec((1, d), lambda i, j: (0, 0))],
        out_specs=pl.BlockSpec((tm, d), lambda i, j: (i, 0)),
        out_shape=jax.ShapeDtypeStruct((r, d), F32),
        scratch_shapes=[pltpu.VMEM((tm, d), BF16), pltpu.VMEM((tm, d), F32)],
        compiler_params=_params(("arbitrary", "arbitrary")),
        name="conv_ffn",
    )(x, g.reshape(1, d), sc, sh, gate, w_up, w_up, conv_w, conv_w, conv_b, conv_b, w_down,
      final_g.reshape(1, d))


def _ones_column_block(rows):
    lane = lax.broadcasted_iota(jnp.int32, (rows, HEAD_DIM), 1)
    return jnp.where(lane == 0, 1.0, 0.0).astype(BF16)


def _diff_attend(q_tiles, k, v1, lam, emit):
    chains = [(t, c) for t in range(len(q_tiles)) for c in range(2)]
    q_cache = {}

    def scores(t, c):
        if t not in q_cache:
            q_cache[t] = q_tiles[t]()
        q = q_cache[t]
        lane = lax.broadcasted_iota(jnp.int32, q.shape, 1)
        qc = jnp.where((lane < DA_HD) == (c == 0), q, 0.0).astype(BF16)
        return _dot_nt(qc, k)

    s = scores(*chains[0])
    first = None
    for idx, (t, c) in enumerate(chains):
        s_next = scores(*chains[idx + 1]) if idx + 1 < len(chains) else None
        e = jnp.exp2(s - jnp.max(s, axis=-1, keepdims=True)).astype(BF16)
        pv = _dot(e, v1)
        o = pv[:, :HEAD_DIM] / pv[:, HEAD_DIM:HEAD_DIM + 1]
        if c == 0:
            first = o
        else:
            emit(t, first - lam * o)
        s = s_next


def _subln(o, w, post_scale):
    return o * lax.rsqrt(jnp.mean(o * o, axis=-1, keepdims=True) + 1e-5) * w * post_scale


def _attn_ctx_kernel(*refs, post_scale, layer_idx, first_layer):
    if first_layer:
        lam_ref, q_ref, k_ref, v_ref, w_ref, o_ref, nk_ref, nv_ref = refs
    else:
        lam_ref, q_ref, k_ref, v_ref, w_ref, _, _, o_ref, nk_ref, nv_ref = refs
    lam = lam_ref[0]
    n = q_ref.shape[0]
    ones = _ones_column_block(n)
    scale = DA_HD ** -0.5 * LOG2E
    for h in range(N_HEADS):
        sl = slice(h * HEAD_DIM, (h + 1) * HEAD_DIM)
        v1 = jnp.concatenate([v_ref[:, sl], ones], axis=1)

        def emit(t, o, sl=sl):
            o_ref[:, sl] = _subln(o, w_ref[...], post_scale).astype(BF16)

        _diff_attend([lambda sl=sl: q_ref[:, sl].astype(F32) * scale], k_ref[:, sl], v1, lam, emit)

    kv_shape = (n, N_HEADS, HEAD_DIM)
    if first_layer:
        for l in range(nk_ref.shape[0]):
            if l == layer_idx:
                nk_ref[l] = k_ref[...].astype(F32).reshape(kv_shape)
                nv_ref[l] = v_ref[...].astype(F32).reshape(kv_shape)
            else:
                nk_ref[l] = jnp.zeros(kv_shape, F32)
                nv_ref[l] = jnp.zeros(kv_shape, F32)
    else:
        nk_ref[...] = k_ref[...].astype(F32).reshape(kv_shape)
        nv_ref[...] = v_ref[...].astype(F32).reshape(kv_shape)


def attn_context(qkv, lam, subln_w, seq_len, post_scale, layer_idx, n_layers, kv_prev):
    r = qkv.shape[0]
    b = r // seq_len
    d = D_MODEL
    first_layer = kv_prev is None
    kern = functools.partial(_attn_ctx_kernel, post_scale=post_scale, layer_idx=layer_idx,
                             first_layer=first_layer)
    in_specs = [pl.BlockSpec(memory_space=pltpu.SMEM),
                pl.BlockSpec((seq_len, d), lambda i: (i, 0)),
                pl.BlockSpec((seq_len, d), lambda i: (i, 1)),
                pl.BlockSpec((seq_len, d), lambda i: (i, 2)),
                pl.BlockSpec((1, HEAD_DIM), lambda i: (0, 0))]
    args = [lam, qkv, qkv, qkv, subln_w.reshape(1, HEAD_DIM)]
    kv_sds = jax.ShapeDtypeStruct((b, n_layers, seq_len, N_HEADS, HEAD_DIM), F32)
    if first_layer:
        kv_spec = pl.BlockSpec((None, n_layers, seq_len, N_HEADS, HEAD_DIM),
                               lambda i: (i, 0, 0, 0, 0))
        aliases = {}
    else:
        kv_spec = pl.BlockSpec((None, None, seq_len, N_HEADS, HEAD_DIM),
                               lambda i: (i, layer_idx, 0, 0, 0))
        in_specs += [pl.BlockSpec(memory_space=pl.ANY)] * 2
        args += list(kv_prev)
        aliases = {5: 1, 6: 2}
    o, nk, nv = pl.pallas_call(
        kern,
        grid=(b,),
        in_specs=in_specs,
        out_specs=[pl.BlockSpec((seq_len, d), lambda i: (i, 0)), kv_spec, kv_spec],
        out_shape=[jax.ShapeDtypeStruct((r, d), BF16), kv_sds, kv_sds],
        input_output_aliases=aliases,
        compiler_params=_params(("arbitrary",)),
        name="attn_context",
    )(*args)
    return o, (nk, nv)


def _rope(x, cos, sin):
    lane = lax.broadcasted_iota(jnp.int32, x.shape, 1)
    partner = jnp.where((lane & 16) == 0, pltpu.roll(x, HEAD_DIM - 16, 1), pltpu.roll(x, 16, 1))
    return x * cos + partner * sin


def _attn_lat_kernel(lam_ref, q_ref, k_ref, v_ref, ck_ref, cv_ref, cos_ref, sin_ref, w_ref, o_ref,
                     kcat_ref, vcat_ref, *, post_scale):
    lam = lam_ref[0]
    past = ck_ref.shape[0]
    n = q_ref.shape[0]
    kcat_ref[0:past, :] = ck_ref[...]
    kcat_ref[past:past + n, :] = _rope(k_ref[...].astype(F32), cos_ref[...],
                                       sin_ref[...]).astype(BF16)
    vcat_ref[0:past, 0:HEAD_DIM] = cv_ref[...]
    vcat_ref[past:past + n, 0:HEAD_DIM] = v_ref[...]
    vcat_ref[:, HEAD_DIM:] = _ones_column_block(past + n)
    scale = DA_HD ** -0.5 * LOG2E

    def q_tile(t):
        rows = slice(t * Q_TILE, (t + 1) * Q_TILE)
        return _rope(q_ref[rows, :].astype(F32), cos_ref[rows, :], sin_ref[rows, :]) * scale

    def emit(t, o):
        o_ref[t * Q_TILE:(t + 1) * Q_TILE, :] = _subln(o, w_ref[...], post_scale).astype(BF16)

    _diff_attend([functools.partial(q_tile, t) for t in range(n // Q_TILE)], kcat_ref[...],
                 vcat_ref[...], lam, emit)


def _flatten_heads_kernel(k_ref, v_ref, ok_ref, ov_ref):
    shape = ok_ref.shape
    ok_ref[...] = k_ref[...].reshape(shape).astype(BF16)
    ov_ref[...] = v_ref[...].reshape(shape).astype(BF16)


def flatten_cache(cache_k, cache_v):
    b, l, past, h, e = cache_k.shape
    in_spec = pl.BlockSpec((None, None, past, h, e), lambda i, j: (i, j, 0, 0, 0))
    out_spec = pl.BlockSpec((None, None, past, h * e), lambda i, j: (i, j, 0, 0))
    sds = jax.ShapeDtypeStruct((b, l, past, h * e), BF16)
    return pl.pallas_call(
        _flatten_heads_kernel,
        grid=(b, l),
        in_specs=[in_spec, in_spec],
        out_specs=[out_spec, out_spec],
        out_shape=[sds, sds],
        compiler_params=_params(("arbitrary", "arbitrary")),
        name="flatten_cache",
    )(cache_k, cache_v)


def attn_latent(qkv, cache_k, cache_v, layer_idx, cos, sin, lam, subln_w, seq_len, post_scale):
    r = qkv.shape[0]
    b = r // seq_len
    past = cache_k.shape[2]
    kern = functools.partial(_attn_lat_kernel, post_scale=post_scale)
    blk = (seq_len, HEAD_DIM)
    cache_spec = pl.BlockSpec((None, None, past, HEAD_DIM), lambda i, h: (i, layer_idx, 0, h))
    tab_spec = pl.BlockSpec((seq_len, HEAD_DIM), lambda i, h: (0, 0))
    return pl.pallas_call(
        kern,
        grid=(b, N_HEADS),
        in_specs=[pl.BlockSpec(memory_space=pltpu.SMEM),
                  pl.BlockSpec(blk, lambda i, h: (i, h)),
                  pl.BlockSpec(blk, lambda i, h: (i, N_HEADS + h)),
                  pl.BlockSpec(blk, lambda i, h: (i, 2 * N_HEADS + h)),
                  cache_spec, cache_spec, tab_spec, tab_spec,
                  pl.BlockSpec((1, HEAD_DIM), lambda i, h: (0, 0))],
        out_specs=pl.BlockSpec(blk, lambda i, h: (i, h)),
        out_shape=jax.ShapeDtypeStruct((r, D_MODEL), BF16),
        scratch_shapes=[pltpu.VMEM((past + seq_len, HEAD_DIM), BF16),
                        pltpu.VMEM((past + seq_len, 2 * HEAD_DIM), BF16)],
        compiler_params=_params(("arbitrary", "arbitrary")),
        name="attn_latent",
    )(lam, qkv, qkv, qkv, cache_k, cache_v, cos, sin, subln_w.reshape(1, HEAD_DIM))


def _prefix_rows(x):
    row = lax.broadcasted_iota(jnp.int32, x.shape, 0)
    for j in range(N_LEVELS):
        s = 1 << j
        x = x + jnp.where(row >= s, pltpu.roll(x, s, 0), 0.0)
    return x


def _block_boundary(x, m):
    n, lanes = x.shape
    w = 2 * m
    if w >= 8:
        y = x.reshape(n // w, w, lanes)
        return jnp.broadcast_to(y[:, m - 1:m, :], y.shape).reshape(n, lanes)
    y = x.reshape(n // 8, 8, lanes)
    sub = lax.broadcasted_iota(jnp.int32, y.shape, 1)
    out = None
    for grp in range(8 // w):
        src = grp * w + m - 1
        b = jnp.broadcast_to(y[:, src:src + 1, :], y.shape)
        out = b if out is None else jnp.where(sub >= grp * w, b, out)
    return out.reshape(n, lanes)


def _scan_constants():
    t = np.arange(CHUNK)
    block = [(t[:, None] >> j) == (t[None, :] >> j) for j in range(N_LEVELS + 1)]
    fwd = [block[0]]
    for j in range(N_LEVELS):
        m = 1 << j
        up = (t & m) != 0
        fwd.append(block[j + 1] & up[:, None] & ~up[None, :])
    pair = np.stack([np.stack(fwd), np.stack([a.T for a in fwd])])
    return dict(block=jnp.asarray(np.stack(block).astype(np.float32)),
                pair=jnp.asarray(pair.astype(np.float32)))


def _gate_norm_out(o, gate, w):
    y = o * lax.rsqrt(jnp.mean(o * o, axis=-1, keepdims=True) + EPS) * w
    return (y * _silu(gate)).astype(BF16)


def _hgrn_chunk(q, z, v, lb, pair_ref, st, rev):
    e = jnp.exp(-jnp.abs(z))
    r = 1.0 / (1.0 + e)
    pos = z >= 0
    sig = jnp.where(pos, r, e * r)
    nsig = jnp.where(pos, e * r, r)
    f = lb + (1.0 - lb) * sig
    key = (1.0 - lb) * nsig
    lf = jnp.log2(f)
    incl = _prefix_rows(lf)
    base = incl - lf if rev else incl
    vb = v.astype(BF16)

    a = pair_ref[0] * _dot_nt(q.astype(BF16), key.astype(BF16))
    for lvl in range(N_LEVELS):
        dm = base - _block_boundary(incl, 1 << lvl)
        em = jnp.exp2(-jnp.abs(dm))
        a = a + pair_ref[lvl + 1] * _dot_nt((q * em).astype(BF16), (key * em).astype(BF16))

    tot = incl[CHUNK - 1:CHUNK, :]
    if rev:
        e_in = jnp.exp2(tot - base)
        e_out = jnp.exp2(base)
    else:
        e_in = jnp.exp2(incl)
        e_out = jnp.exp2(tot - incl)
    o = _dot(a.astype(BF16), vb) + _dot_nt((q * e_in).astype(BF16), st.astype(BF16))
    st_new = jnp.exp2(tot) * st + _dot_tn(vb, (key * e_out).astype(BF16))
    return o, st_new


def _hgrn_kernel(*refs, has_init, emit_state):
    refs = list(refs)
    q_ref, zf_ref, zb_ref, i_ref, g_ref, lb_ref, w_ref, pair_ref = refs[:8]
    refs = refs[8:]
    s0_ref = refs.pop(0) if has_init else None
    o_ref = refs.pop(0)
    s_out_ref = refs.pop(0) if emit_state else None
    of_ref, ob_ref, st_ref = refs

    n = q_ref.shape[0]
    nc = n // CHUNK
    lb = lb_ref[...]
    for d in range(2):
        st_ref[d] = s0_ref[d].T if has_init else jnp.zeros((HEAD_DIM, HEAD_DIM), F32)

    def body(c, carry):
        rf = pl.ds(pl.multiple_of(c * CHUNK, CHUNK), CHUNK)
        rb = pl.ds(pl.multiple_of((nc - 1 - c) * CHUNK, CHUNK), CHUNK)
        scale = HEAD_DIM ** -0.5
        load = lambda ref, rows: ref[rows, :].astype(F32)
        o, s = _hgrn_chunk(load(q_ref, rf) * scale, load(zf_ref, rf), load(i_ref, rf), lb,
                           pair_ref.at[0], st_ref[0], False)
        of_ref[rf, :] = o
        st_ref[0] = s
        o, s = _hgrn_chunk(load(q_ref, rb) * scale, load(zb_ref, rb), load(i_ref, rb), lb,
                           pair_ref.at[1], st_ref[1], True)
        ob_ref[rb, :] = o
        st_ref[1] = s
        return carry

    lax.fori_loop(0, nc, body, 0)
    o_ref[...] = _gate_norm_out(of_ref[...] + ob_ref[...], g_ref[...].astype(F32), w_ref[...])
    if emit_state:
        for d in range(2):
            s_out_ref[d] = st_ref[d].T


def hgrn_scan(proj, lb, norm_w, consts, seq_len, state_in, layer_idx, emit_state):
    r = proj.shape[0]
    b = r // seq_len
    has_init = state_in is not None
    blk = (seq_len, HEAD_DIM)
    col = lambda k: pl.BlockSpec(blk, lambda i, h, k=k: (i, k * N_HEADS + h))
    state_spec = pl.BlockSpec((None, None, 2, None, HEAD_DIM, HEAD_DIM),
                              lambda i, h: (i, layer_idx, 0, h, 0, 0))
    in_specs = [col(0), col(1), col(2), col(3), col(4),
                pl.BlockSpec((None, 1, HEAD_DIM), lambda i, h: (h, 0, 0)),
                pl.BlockSpec((1, HEAD_DIM), lambda i, h: (0, 0)),
                pl.BlockSpec((2, N_LEVELS + 1, CHUNK, CHUNK), lambda i, h: (0, 0, 0, 0))]
    args = [proj, proj, proj, proj, proj, lb.reshape(N_HEADS, 1, HEAD_DIM),
            norm_w.reshape(1, HEAD_DIM), consts['pair']]
    if has_init:
        in_specs.append(state_spec)
        args.append(state_in)
    out_specs = [pl.BlockSpec(blk, lambda i, h: (i, h))]
    out_shape = [jax.ShapeDtypeStruct((r, D_MODEL), BF16)]
    if emit_state:
        out_specs.append(pl.BlockSpec((None, None, 2, None, HEAD_DIM, HEAD_DIM),
                                      lambda i, h: (i, 0, 0, h, 0, 0)))
        out_shape.append(jax.ShapeDtypeStruct((b, 1, 2, N_HEADS, HEAD_DIM, HEAD_DIM), F32))
    kern = functools.partial(_hgrn_kernel, has_init=has_init, emit_state=emit_state)
    res = pl.pallas_call(
        kern,
        grid=(b, N_HEADS),
        in_specs=in_specs,
        out_specs=out_specs,
        out_shape=out_shape,
        scratch_shapes=[pltpu.VMEM(blk, F32), pltpu.VMEM(blk, F32),
                        pltpu.VMEM((2, HEAD_DIM, HEAD_DIM), F32)],
        compiler_params=_params(("arbitrary", "arbitrary")),
        name="hgrn_scan",
    )(*args)
    return res if emit_state else (res[0], None)


def _lane_column(x, lane_idx):
    lane = lax.broadcasted_iota(jnp.int32, x.shape, 1)
    colv = jnp.sum(jnp.where(lane == lane_idx, x, 0.0), axis=1, keepdims=True)
    return jnp.broadcast_to(colv, x.shape)


def _gdn_prepare(items, block_ref):
    row = lax.broadcasted_iota(jnp.int32, (CHUNK, CHUNK), 0)
    colm = lax.broadcasted_iota(jnp.int32, (CHUNK, CHUNK), 1)
    pre = []
    for q, k, v, g, beta, kk, qk, rev in items:
        g_end = g[0:1, :] if rev else g[CHUNK - 1:CHUNK, :]
        within = (row <= colm) if rev else (row >= colm)
        strict = (row < colm) if rev else (row > colm)
        decay = jnp.exp2(jnp.where(within, g - g.T, -jnp.inf))
        a = jnp.where(strict, beta * kk * decay, 0.0)
        eg = jnp.exp2(g)
        x = jnp.concatenate([v * beta, k * (beta * eg)], axis=1).astype(BF16)
        p = (qk * decay).astype(BF16)
        ke = (k * jnp.exp2(g_end - g)).astype(BF16)
        pre.append((a, x, p, ke, q * eg, jnp.exp2(g_end)))
    ts = [block_ref[0] - it[0] * (block_ref[1] - block_ref[0]) for it in pre]
    for lvl in range(1, N_LEVELS):
        sel = block_ref[lvl + 1] - block_ref[lvl]
        tbs = [t.astype(BF16) for t in ts]
        lts = [_dot((it[0] * sel).astype(BF16), tb).astype(BF16) for it, tb in zip(pre, tbs)]
        ts = [t - _dot(tb, lt) for t, tb, lt in zip(ts, tbs, lts)]
    wus = [_dot(t.astype(BF16), it[1]).astype(BF16) for t, it in zip(ts, pre)]
    pwus = [_dot(it[2], wu) for it, wu in zip(pre, wus)]
    kwus = [_dot_tn(it[3], wu) for it, wu in zip(pre, wus)]
    out = []
    for it, pwu, kwu in zip(pre, pwus, kwus):
        qs = (it[4] - pwu[:, HEAD_DIM:]).astype(BF16)
        out.append((qs, pwu[:, :HEAD_DIM], kwu[:, HEAD_DIM:].astype(BF16), kwu[:, :HEAD_DIM], it[5]))
    return out


def _gdn_kernel(*refs, has_init, emit_state, hp):
    refs = list(refs)
    (q_ref, k_ref, v_ref, gate_ref, ab_ref, cq_ref, ck_ref, cv_ref, rate_ref, bias_ref,
     w_ref, block_ref) = refs[:12]
    refs = refs[12:]
    s0_ref = refs.pop(0) if has_init else None
    o_ref = refs.pop(0)
    s_out_ref = refs.pop(0) if emit_state else None
    qn_ref, kn_ref, vn_ref, qs_ref, ms_ref, o_acc_ref, b_ref, dec_ref, st_ref = refs

    n = q_ref.shape[0]
    nc = n // CHUNK
    group = max(1, min(nc, GDN_CHAINS // (2 * hp)))
    head0 = pl.program_id(1) * hp
    pos = lax.broadcasted_iota(jnp.int32, (n, hp * HEAD_DIM), 0)

    def conv_silu(x_ref, cw_ref):
        x = x_ref[...].astype(F32)
        prev = jnp.where(pos == 0, 0.0, pltpu.roll(x, 1, 0))
        nxt = jnp.where(pos == n - 1, 0.0, pltpu.roll(x, n - 1, 0))
        return _silu(prev * cw_ref[0:1, :] + x * cw_ref[1:2, :] + nxt * cw_ref[2:3, :])

    def l2norm(x):
        return x * lax.rsqrt(jnp.sum(x * x, axis=-1, keepdims=True) + 1e-6)

    qc = conv_silu(q_ref, cq_ref)
    kc = conv_silu(k_ref, ck_ref)
    vn_ref[...] = conv_silu(v_ref, cv_ref)
    for hh in range(hp):
        sl = slice(hh * HEAD_DIM, (hh + 1) * HEAD_DIM)
        qn_ref[:, sl] = l2norm(qc[:, sl]) * (HEAD_DIM ** -0.5)
        kn_ref[:, sl] = l2norm(kc[:, sl])
        for d in range(2):
            st_ref[d, hh] = s0_ref[d, hh] if has_init else jnp.zeros((HEAD_DIM, HEAD_DIM), F32)

    def chunk_rows(c):
        return pl.ds(pl.multiple_of(c * CHUNK, CHUNK), CHUNK)

    def prepare_body(i, carry):
        items, dest = [], []
        lane = lax.broadcasted_iota(jnp.int32, (CHUNK, HEAD_DIM), 1)
        for grp in range(group):
            rows = chunk_rows(i * group + grp)
            ab = ab_ref[rows, :]
            la = -(rate_ref[...] * LOG2E) * jax.nn.softplus(ab + bias_ref[...])
            sg_all = jax.nn.sigmoid(ab)
            incl = _prefix_rows(la)
            tot = incl[CHUNK - 1:CHUNK, :]
            g_all = jnp.where(lane >= N_HEADS, tot - incl + la, incl)
            for hh in range(hp):
                sl = slice(hh * HEAD_DIM, (hh + 1) * HEAD_DIM)
                q = qn_ref[rows, sl]
                k = kn_ref[rows, sl]
                v = vn_ref[rows, sl]
                kb = k.astype(BF16)
                kk = _dot_nt(kb, kb)
                qk = _dot_nt(q.astype(BF16), kb)
                for d in range(2):
                    g = _lane_column(g_all, d * N_HEADS + head0 + hh)
                    beta = _lane_column(sg_all, (2 + d) * N_HEADS + head0 + hh)
                    items.append((q, k, v, g, beta, kk, qk, d == 1))
                    dest.append((d, hh, rows))
        for (d, hh, rows), (qs, o0, ms, b, dec) in zip(dest, _gdn_prepare(items, block_ref)):
            qs_ref[d, hh, rows, :] = qs
            ms_ref[d, hh, rows, :] = ms
            o_acc_ref[d, hh, rows, :] = o0
            b_ref[d, hh, rows, :] = b
            dec_ref[d, hh, rows, :] = jnp.broadcast_to(dec, (CHUNK, HEAD_DIM))
        return carry

    lax.fori_loop(0, nc // group, prepare_body, 0)

    def scan_body(c, carry):
        for hh in range(hp):
            for d, cc in ((0, c), (1, nc - 1 - c)):
                rows = chunk_rows(cc)
                s = st_ref[d, hh]
                sb = s.astype(BF16)
                o_acc_ref[d, hh, rows, :] += _dot(qs_ref[d, hh, rows, :], sb)
                st_ref[d, hh] = (dec_ref[d, hh, rows, :] * s - _dot(ms_ref[d, hh, rows, :], sb)
                                 + b_ref[d, hh, rows, :])
        return carry

    lax.fori_loop(0, nc, scan_body, 0)
    for hh in range(hp):
        sl = slice(hh * HEAD_DIM, (hh + 1) * HEAD_DIM)
        o_ref[:, sl] = _gate_norm_out(o_acc_ref[0, hh] + o_acc_ref[1, hh],
                                      gate_ref[:, sl].astype(F32), w_ref[...])
    if emit_state:
        for d in range(2):
            for hh in range(hp):
                s_out_ref[d, hh] = st_ref[d, hh]


def gdn_scan(proj, ab, conv_w, rate, bias, norm_w, consts, seq_len, state_in, layer_idx,
             emit_state):
    r = proj.shape[0]
    b = r // seq_len
    has_init = state_in is not None
    nc = seq_len // CHUNK
    hp = max(1, min(N_HEADS, GDN_CHAINS // (2 * nc)))
    blk = (seq_len, hp * HEAD_DIM)
    nhb = N_HEADS // hp
    col = lambda k: pl.BlockSpec(blk, lambda i, h, k=k: (i, k * nhb + h))
    ccol = lambda k: pl.BlockSpec((3, hp * HEAD_DIM), lambda i, h, k=k: (0, k * nhb + h))
    vec = pl.BlockSpec((1, HEAD_DIM), lambda i, h: (0, 0))
    state_blk = (None, None, 2, hp, HEAD_DIM, HEAD_DIM)
    in_specs = [col(0), col(1), col(2), col(3),
                pl.BlockSpec((seq_len, HEAD_DIM), lambda i, h: (i, 0)),
                ccol(0), ccol(1), ccol(2), vec, vec, vec,
                pl.BlockSpec((N_LEVELS + 1, CHUNK, CHUNK), lambda i, h: (0, 0, 0))]
    args = [proj, proj, proj, proj, ab, conv_w, conv_w, conv_w, rate, bias,
            norm_w.reshape(1, HEAD_DIM), consts['block']]
    if has_init:
        in_specs.append(pl.BlockSpec(state_blk, lambda i, h: (i, layer_idx, 0, h, 0, 0)))
        args.append(state_in)
    out_specs = [pl.BlockSpec(blk, lambda i, h: (i, h))]
    out_shape = [jax.ShapeDtypeStruct((r, D_MODEL), BF16)]
    if emit_state:
        out_specs.append(pl.BlockSpec(state_blk, lambda i, h: (i, 0, 0, h, 0, 0)))
        out_shape.append(jax.ShapeDtypeStruct((b, 1, 2, N_HEADS, HEAD_DIM, HEAD_DIM), F32))
    kern = functools.partial(_gdn_kernel, has_init=has_init, emit_state=emit_state, hp=hp)
    per_head = (2, hp, seq_len, HEAD_DIM)
    res = pl.pallas_call(
        kern,
        grid=(b, nhb),
        in_specs=in_specs,
        out_specs=out_specs,
        out_shape=out_shape,
        scratch_shapes=[pltpu.VMEM(blk, F32)] * 3
        + [pltpu.VMEM(per_head, BF16)] * 2 + [pltpu.VMEM(per_head, F32)] * 3
        + [pltpu.VMEM((2, hp, HEAD_DIM, HEAD_DIM), F32)],
        compiler_params=_params(("arbitrary", "arbitrary")),
        name="gdn_scan",
    )(*args)
    return res if emit_state else (res[0], None)


def _rope_tables(n):
    rows = n // GRID_W
    row = jnp.repeat(jnp.arange(rows, dtype=F32), GRID_W)
    colp = jnp.tile(jnp.arange(GRID_W, dtype=F32), rows)
    nf = DA_HD // 4
    inv = ROPE_BASE ** (-jnp.arange(nf, dtype=F32) / nf)
    ar = row[:, None] * inv
    ac = colp[:, None] * inv
    cr, sr, cc, sc = jnp.cos(ar), jnp.sin(ar), jnp.cos(ac), jnp.sin(ac)
    cos = jnp.concatenate([cr, cr, cc, cc] * 2, axis=1)
    sin = jnp.concatenate([-sr, sr, -sc, sc] * 2, axis=1)
    return cos, sin


def _trunk(x, mods, rows_per_mod, seq_len, P, cache):
    ctx_mode = cache is None
    depth = P['w_mod'].shape[0]
    n_attn = P['attn_w_in'].shape[0]
    kv, hgrn_state, gdn_state = None, None, None
    for i in range(depth):
        sh1, sc1, g1, sh2, sc2, g2 = mods[i]
        kind, j = i % 3, i // 3
        proj = functools.partial(norm_proj, x, P['norm_g'][i, 0], sc1, sh1,
                                 rows_per_mod=rows_per_mod)
        if kind == 0:
            qkv = proj(P['attn_w_in'][j], tn=1024, out_dtype=BF16)
            lam_init = 0.8 - 0.6 * math.exp(-0.3 * i)
            lp = P['attn_lambda'][j]
            lam = (jnp.exp(jnp.sum(lp[0] * lp[1])) - jnp.exp(jnp.sum(lp[2] * lp[3]))
                   + lam_init).reshape(1)
            if ctx_mode:
                o, kv = attn_context(qkv, lam, P['attn_subln'][j], seq_len, 1.0 - lam_init, j,
                                     n_attn, kv)
            else:
                o = attn_latent(qkv, cache['attn_k'], cache['attn_v'], j, cache['cos'],
                                cache['sin'], lam, P['attn_subln'][j], seq_len, 1.0 - lam_init)
            w_out = P['attn_w_out'][j]
        elif kind == 1:
            o, st = hgrn_scan(proj(P['hgrn_w_in'][j], tn=1024, out_dtype=BF16), P['lb_all'][i],
                              P['hgrn_norm'][j], P['consts'], seq_len,
                              None if ctx_mode else cache['hgrn'], j, ctx_mode)
            if ctx_mode:
                hgrn_state = st
            w_out = P['hgrn_w_out'][j]
        else:
            o, st = gdn_scan(proj(P['gdn_w_main'][j], tn=1024, out_dtype=BF16),
                             proj(P['gdn_w_tail'][j], tn=HEAD_DIM, out_dtype=F32),
                             P['gdn_conv'][j], P['gdn_rate'][j], P['gdn_bias'][j],
                             P['gdn_norm'][j], P['consts'], seq_len,
                             None if ctx_mode else cache['gdn'], j, ctx_mode)
            if ctx_mode:
                gdn_state = st
            w_out = P['gdn_w_out'][j]
        x = out_proj_residual(o, w_out, x, g1, rows_per_mod)
        x = conv_ffn_residual(x, P['norm_g'][i, 1], sc2, sh2, g2, P['ffn_w_up'][i],
                              P['ffn_conv'][i], P['ffn_conv_b'][i], P['ffn_w_down'][i],
                              P['final_g'], rows_per_mod, seq_len, i == depth - 1)
    return x, (kv, hgrn_state, gdn_state)


def kernel(x_prompt, x_sample, cache_attn_k, cache_attn_v, state_hgrn, state_gdn, c, c_ctx, norm_g, w_mod, b_mod, final_g, attn_w_in, attn_lambda, attn_subln, attn_w_out, hgrn_w_in, hgrn_lb, hgrn_norm, hgrn_w_out, gdn_w_in, gdn_conv, gdn_a_log, gdn_dt_bias, gdn_norm, gdn_w_out, ffn_w_up, ffn_conv, ffn_conv_b, ffn_w_down):
    batch, seq, d = x_prompt.shape
    dec_batch, dec_seq, _ = x_sample.shape
    depth = w_mod.shape[0]
    n_gdn = gdn_w_in.shape[0]

    lb_all = jnp.cumsum(jax.nn.softmax(hgrn_lb, axis=0), axis=0)
    lb_all = lb_all - lb_all[0]

    pad = jnp.zeros((n_gdn, HEAD_DIM - 2 * N_HEADS), F32)
    gdn_rate = jnp.concatenate([jnp.exp(gdn_a_log).reshape(n_gdn, 2 * N_HEADS), pad], axis=1)
    gdn_bias = jnp.concatenate([gdn_dt_bias.reshape(n_gdn, 2 * N_HEADS), pad], axis=1)
    tail = gdn_w_in[:, :, 4 * d:]
    tail = jnp.concatenate([tail, jnp.zeros((n_gdn, d, HEAD_DIM - tail.shape[2]), F32)], axis=2)

    P = dict(norm_g=norm_g, w_mod=w_mod, final_g=final_g,
             attn_w_in=attn_w_in.astype(BF16), attn_lambda=attn_lambda, attn_subln=attn_subln,
             attn_w_out=attn_w_out.astype(BF16),
             hgrn_w_in=hgrn_w_in.astype(BF16), lb_all=lb_all, hgrn_norm=hgrn_norm,
             hgrn_w_out=hgrn_w_out.astype(BF16),
             gdn_w_main=gdn_w_in[:, :, :4 * d].astype(BF16), gdn_w_tail=tail.astype(BF16),
             gdn_conv=gdn_conv, gdn_rate=gdn_rate.reshape(n_gdn, 1, HEAD_DIM),
             gdn_bias=gdn_bias.reshape(n_gdn, 1, HEAD_DIM), gdn_norm=gdn_norm,
             gdn_w_out=gdn_w_out.astype(BF16),
             ffn_w_up=ffn_w_up.astype(BF16), ffn_conv=ffn_conv, ffn_conv_b=ffn_conv_b,
             ffn_w_down=ffn_w_down.astype(BF16), consts=_scan_constants())

    cond = jnp.zeros((16, d), F32).at[:dec_batch].set(c).at[dec_batch].set(c_ctx)
    mod = modulation_all(cond, w_mod, b_mod).reshape(depth, 16, 6, 1, d)
    mods_lat = [[mod[i, :dec_batch, t] for t in range(6)] for i in range(depth)]
    mods_ctx = [[mod[i, dec_batch:dec_batch + 1, t] for t in range(6)] for i in range(depth)]

    y_prompt, ((new_k, new_v), new_hgrn, new_gdn) = _trunk(
        x_prompt.reshape(batch * seq, d), mods_ctx, batch * seq, seq, P, None)

    cos, sin = _rope_tables(dec_seq)
    flat_k, flat_v = flatten_cache(cache_attn_k, cache_attn_v)
    cache = dict(attn_k=flat_k, attn_v=flat_v, hgrn=state_hgrn, gdn=state_gdn, cos=cos, sin=sin)
    y_sample, _ = _trunk(x_sample.reshape(dec_batch * dec_seq, d), mods_lat, dec_seq, dec_seq, P,
                         cache)

    return (y_prompt.reshape(batch, seq, d), y_sample.reshape(dec_batch, dec_seq, d),
            new_k, new_v, new_hgrn, new_gdn)
```

```python
import functools
import math

import jax
import jax.numpy as jnp
import numpy as np
from jax import lax
from jax.experimental import pallas as pl
from jax.experimental.pallas import tpu as pltpu

F32 = jnp.float32
BF16 = jnp.bfloat16

D_MODEL = 1024
N_HEADS = 8
HEAD_DIM = 128
DA_HD = 64
GRID_W = 64
ROPE_BASE = 10000.0
D_FF = 2816
EPS = 1e-6
LOG2E = 1.4426950408889634
CHUNK = 128
N_LEVELS = 7
ROW_TILE = 1024
FF_TILE = 256
FFN_TILES_PER_STEP = 6
Q_TILE = 256
SCORE_LOOKAHEAD = 8
GDN_CHAINS = 16
GDN_HEADS_PER_STEP = 4
VMEM_LIMIT = 56 * 1024 * 1024


def _params(sem, vmem=VMEM_LIMIT):
    return pltpu.CompilerParams(dimension_semantics=sem, vmem_limit_bytes=vmem)


def _dot(a, b):
    return jnp.dot(a, b, preferred_element_type=F32)


def _dot_nt(a, b):
    return lax.dot_general(a, b, (((1,), (1,)), ((), ())), preferred_element_type=F32)


def _dot_tn(a, b):
    return lax.dot_general(a, b, (((0,), (0,)), ((), ())), preferred_element_type=F32)


def _silu(x):
    return x * jax.nn.sigmoid(x)


def _norm_mod(x, g, sc, sh):
    y = x * lax.rsqrt(jnp.mean(x * x, axis=-1, keepdims=True) + EPS)
    return (y * g) * (1.0 + sc) + sh


def _mod_kernel(c_ref, w_ref, b_ref, o_ref):
    s = _silu(c_ref[...]).astype(BF16)
    o_ref[...] = _dot(s, w_ref[...].astype(BF16)) + b_ref[...]


def modulation_all(cond, w_mod, b_mod):
    depth, d, n = w_mod.shape
    tn = 1024
    return pl.pallas_call(
        _mod_kernel,
        grid=(depth, n // tn),
        in_specs=[pl.BlockSpec((16, d), lambda l, j: (0, 0)),
                  pl.BlockSpec((None, d, tn), lambda l, j: (l, 0, j)),
                  pl.BlockSpec((None, 1, tn), lambda l, j: (l, 0, j))],
        out_specs=pl.BlockSpec((None, 16, tn), lambda l, j: (l, 0, j)),
        out_shape=jax.ShapeDtypeStruct((depth, 16, n), F32),
        compiler_params=_params(("arbitrary", "arbitrary")),
        name="modulation",
    )(cond, w_mod, b_mod.reshape(depth, 1, n))


def _proj_kernel(x_ref, g_ref, sc_ref, sh_ref, w_ref, o_ref, h_ref):
    @pl.when(pl.program_id(1) == 0)
    def _():
        h_ref[...] = _norm_mod(x_ref[...], g_ref[...], sc_ref[...], sh_ref[...]).astype(BF16)

    o_ref[...] = _dot(h_ref[...], w_ref[...]).astype(o_ref.dtype)


def norm_proj(x, g, sc, sh, w, rows_per_mod, tn, out_dtype):
    r, d = x.shape
    n = w.shape[1]
    tm = ROW_TILE
    mod_spec = pl.BlockSpec((None, 1, d), lambda i, j: ((i * tm) // rows_per_mod, 0, 0))
    return pl.pallas_call(
        _proj_kernel,
        grid=(r // tm, n // tn),
        in_specs=[pl.BlockSpec((tm, d), lambda i, j: (i, 0)),
                  pl.BlockSpec((1, d), lambda i, j: (0, 0)),
                  mod_spec, mod_spec,
                  pl.BlockSpec((d, tn), lambda i, j: (0, j))],
        out_specs=pl.BlockSpec((tm, tn), lambda i, j: (i, j)),
        out_shape=jax.ShapeDtypeStruct((r, n), out_dtype),
        scratch_shapes=[pltpu.VMEM((tm, d), BF16)],
        compiler_params=_params(("arbitrary", "arbitrary")),
        name="norm_proj",
    )(x, g.reshape(1, d), sc, sh, w)


def _out_proj_kernel(o_ref, w_ref, x_ref, gate_ref, y_ref):
    y_ref[...] = x_ref[...] + gate_ref[...] * _dot(o_ref[...], w_ref[...])


def out_proj_residual(o, w, x, gate, rows_per_mod):
    r, d = x.shape
    tm = ROW_TILE
    return pl.pallas_call(
        _out_proj_kernel,
        grid=(r // tm,),
        in_specs=[pl.BlockSpec((tm, d), lambda i: (i, 0)),
                  pl.BlockSpec((d, d), lambda i: (0, 0)),
                  pl.BlockSpec((tm, d), lambda i: (i, 0)),
                  pl.BlockSpec((None, 1, d), lambda i: ((i * tm) // rows_per_mod, 0, 0))],
        out_specs=pl.BlockSpec((tm, d), lambda i: (i, 0)),
        out_shape=jax.ShapeDtypeStruct((r, d), F32),
        compiler_params=_params(("arbitrary",)),
        name="out_proj",
    )(o, w, x, gate)


def _ffn_kernel(*refs, seq_len, final_norm):
    x_ref, g_ref, sc_ref, sh_ref, gate_ref = refs[:5]
    nt = FFN_TILES_PER_STEP
    wv_refs, wg_refs, cv_refs, cg_refs, bv_refs, bg_refs, wd_refs = (
        refs[5 + k * nt:5 + (k + 1) * nt] for k in range(7))
    fg_ref, y_ref, h_ref, acc_ref = refs[5 + 7 * nt:]
    j = pl.program_id(1)

    @pl.when(j == 0)
    def _():
        h_ref[...] = _norm_mod(x_ref[...], g_ref[...], sc_ref[...], sh_ref[...]).astype(BF16)
        acc_ref[...] = jnp.zeros_like(acc_ref)

    tm = h_ref.shape[0]
    h = h_ref[...]
    pos = lax.broadcasted_iota(jnp.int32, (tm, FF_TILE), 0) & (seq_len - 1)
    first = pos == 0
    last = pos == seq_len - 1

    def conv(u, cw_ref, b_ref):
        prev = jnp.where(first, 0.0, pltpu.roll(u, 1, 0))
        nxt = jnp.where(last, 0.0, pltpu.roll(u, tm - 1, 0))
        return prev * cw_ref[0:1, :] + u * cw_ref[1:2, :] + nxt * cw_ref[2:3, :] + b_ref[...]

    ups = [(_dot(h, wv_refs[t][...]), _dot(h, wg_refs[t][...])) for t in range(nt)]
    for t in range(nt):
        val = conv(ups[t][0], cv_refs[t], bv_refs[t])
        gte = conv(ups[t][1], cg_refs[t], bg_refs[t])
        act = (val * _silu(gte)).astype(BF16)
        acc_ref[...] += _dot(act, wd_refs[t][...])

    @pl.when(j == pl.num_programs(1) - 1)
    def _():
        y = x_ref[...] + gate_ref[...] * acc_ref[...]
        if final_norm:
            y = y * lax.rsqrt(jnp.mean(y * y, axis=-1, keepdims=True) + EPS) * fg_ref[...]
        y_ref[...] = y


def conv_ffn_residual(x, g, sc, sh, gate, w_up, conv_w, conv_b, w_down, final_g, rows_per_mod,
                      seq_len, final_norm):
    r, d = x.shape
    tm = ROW_TILE
    nt = FFN_TILES_PER_STEP
    nf = w_down.shape[0] // FF_TILE
    mod_spec = pl.BlockSpec((None, 1, d), lambda i, j: ((i * tm) // rows_per_mod, 0, 0))
    kern = functools.partial(_ffn_kernel, seq_len=seq_len, final_norm=final_norm)
    conv_b = conv_b.reshape(1, -1)

    def cols(rows, half):
        return [pl.BlockSpec((rows, FF_TILE), lambda i, j, t=t: (0, half * nf + nt * j + t))
                for t in range(nt)]

    in_specs = ([pl.BlockSpec((tm, d), lambda i, j: (i, 0)),
                 pl.BlockSpec((1, d), lambda i, j: (0, 0)),
                 mod_spec, mod_spec, mod_spec]
                + cols(d, 0) + cols(d, 1) + cols(3, 0) + cols(3, 1) + cols(1, 0) + cols(1, 1)
                + [pl.BlockSpec((FF_TILE, d), lambda i, j, t=t: (nt * j + t, 0)) for t in range(nt)]
                + [pl.BlockSpec((1, d), lambda i, j: (0, 0))])
    args = ([x, g.reshape(1, d), sc, sh, gate] + [w_up] * (2 * nt) + [conv_w] * (2 * nt)
            + [conv_b] * (2 * nt) + [w_down] * nt + [final_g.reshape(1, d)])
    return pl.pallas_call(
        kern,
        grid=(r // tm, nf // nt),
        in_specs=in_specs,
        out_specs=pl.BlockSpec((tm, d), lambda i, j: (i, 0)),
        out_shape=jax.ShapeDtypeStruct((r, d), F32),
        scratch_shapes=[pltpu.VMEM((tm, d), BF16), pltpu.VMEM((tm, d), F32)],
        compiler_params=_params(("arbitrary", "arbitrary")),
        name="conv_ffn",
    )(*args)


def _ones_column_block(rows):
    lane = lax.broadcasted_iota(jnp.int32, (rows, HEAD_DIM), 1)
    return jnp.where(lane == 0, 1.0, 0.0).astype(BF16)


def _diff_attend(q_tiles, k, v1, lam, emit):
    chains = [(t, c) for t in range(len(q_tiles)) for c in range(2)]
    q_cache = {}

    def scores(t, c):
        if t not in q_cache:
            q_cache[t] = q_tiles[t]()
        q = q_cache[t]
        lane = lax.broadcasted_iota(jnp.int32, q.shape, 1)
        qc = jnp.where((lane < DA_HD) == (c == 0), q, 0.0).astype(BF16)
        return _dot_nt(qc, k)

    ahead = SCORE_LOOKAHEAD
    pending = [scores(*ch) for ch in chains[:ahead]]
    first = None
    for idx, (t, c) in enumerate(chains):
        if idx + ahead < len(chains):
            pending.append(scores(*chains[idx + ahead]))
        s = pending.pop(0)
        e = jnp.exp2(s - jnp.max(s, axis=-1, keepdims=True)).astype(BF16)
        pv = _dot(e, v1)
        o = pv[:, :HEAD_DIM] / pv[:, HEAD_DIM:HEAD_DIM + 1]
        if c == 0:
            first = o
        else:
            emit(t, first - lam * o)


def _subln(o, w, post_scale):
    return o * lax.rsqrt(jnp.mean(o * o, axis=-1, keepdims=True) + 1e-5) * w * post_scale


def _attn_ctx_kernel(*refs, post_scale, layer_idx, first_layer):
    if first_layer:
        lam_ref, q_ref, k_ref, v_ref, w_ref, o_ref, nk_ref, nv_ref = refs
    else:
        lam_ref, q_ref, k_ref, v_ref, w_ref, _, _, o_ref, nk_ref, nv_ref = refs
    lam = lam_ref[0]
    n = q_ref.shape[0]
    ones = _ones_column_block(n)
    scale = DA_HD ** -0.5 * LOG2E
    for h in range(N_HEADS):
        sl = slice(h * HEAD_DIM, (h + 1) * HEAD_DIM)
        v1 = jnp.concatenate([v_ref[:, sl], ones], axis=1)

        def emit(t, o, sl=sl):
            o_ref[:, sl] = _subln(o, w_ref[...], post_scale).astype(BF16)

        _diff_attend([lambda sl=sl: q_ref[:, sl].astype(F32) * scale], k_ref[:, sl], v1, lam, emit)

    kv_shape = (n, N_HEADS, HEAD_DIM)
    if first_layer:
        for l in range(nk_ref.shape[0]):
            if l == layer_idx:
                nk_ref[l] = k_ref[...].astype(F32).reshape(kv_shape)
                nv_ref[l] = v_ref[...].astype(F32).reshape(kv_shape)
            else:
                nk_ref[l] = jnp.zeros(kv_shape, F32)
                nv_ref[l] = jnp.zeros(kv_shape, F32)
    else:
        nk_ref[...] = k_ref[...].astype(F32).reshape(kv_shape)
        nv_ref[...] = v_ref[...].astype(F32).reshape(kv_shape)


def attn_context(qkv, lam, subln_w, seq_len, post_scale, layer_idx, n_layers, kv_prev):
    r = qkv.shape[0]
    b = r // seq_len
    d = D_MODEL
    first_layer = kv_prev is None
    kern = functools.partial(_attn_ctx_kernel, post_scale=post_scale, layer_idx=layer_idx,
                             first_layer=first_layer)
    in_specs = [pl.BlockSpec(memory_space=pltpu.SMEM),
                pl.BlockSpec((seq_len, d), lambda i: (i, 0)),
                pl.BlockSpec((seq_len, d), lambda i: (i, 1)),
                pl.BlockSpec((seq_len, d), lambda i: (i, 2)),
                pl.BlockSpec((1, HEAD_DIM), lambda i: (0, 0))]
    args = [lam, qkv, qkv, qkv, subln_w.reshape(1, HEAD_DIM)]
    kv_sds = jax.ShapeDtypeStruct((b, n_layers, seq_len, N_HEADS, HEAD_DIM), F32)
    if first_layer:
        kv_spec = pl.BlockSpec((None, n_layers, seq_len, N_HEADS, HEAD_DIM),
                               lambda i: (i, 0, 0, 0, 0))
        aliases = {}
    else:
        kv_spec = pl.BlockSpec((None, None, seq_len, N_HEADS, HEAD_DIM),
                               lambda i: (i, layer_idx, 0, 0, 0))
        in_specs += [pl.BlockSpec(memory_space=pl.ANY)] * 2
        args += list(kv_prev)
        aliases = {5: 1, 6: 2}
    o, nk, nv = pl.pallas_call(
        kern,
        grid=(b,),
        in_specs=in_specs,
        out_specs=[pl.BlockSpec((seq_len, d), lambda i: (i, 0)), kv_spec, kv_spec],
        out_shape=[jax.ShapeDtypeStruct((r, d), BF16), kv_sds, kv_sds],
        input_output_aliases=aliases,
        compiler_params=_params(("arbitrary",)),
        name="attn_context",
    )(*args)
    return o, (nk, nv)


def _rope(x, cos, sin):
    lane = lax.broadcasted_iota(jnp.int32, x.shape, 1)
    partner = jnp.where((lane & 16) == 0, pltpu.roll(x, HEAD_DIM - 16, 1), pltpu.roll(x, 16, 1))
    return x * cos + partner * sin


def _attn_lat_kernel(lam_ref, q_ref, k_ref, v_ref, ck_ref, cv_ref, cos_ref, sin_ref, w_ref, o_ref,
                     kcat_ref, vcat_ref, *, post_scale):
    lam = lam_ref[0]
    past = ck_ref.shape[0]
    n = q_ref.shape[0]
    kcat_ref[0:past, :] = ck_ref[...]
    kcat_ref[past:past + n, :] = _rope(k_ref[...].astype(F32), cos_ref[...],
                                       sin_ref[...]).astype(BF16)
    vcat_ref[0:past, 0:HEAD_DIM] = cv_ref[...]
    vcat_ref[past:past + n, 0:HEAD_DIM] = v_ref[...]
    vcat_ref[:, HEAD_DIM:] = _ones_column_block(past + n)
    scale = DA_HD ** -0.5 * LOG2E

    def q_tile(t):
        rows = slice(t * Q_TILE, (t + 1) * Q_TILE)
        return _rope(q_ref[rows, :].astype(F32), cos_ref[rows, :], sin_ref[rows, :]) * scale

    def emit(t, o):
        o_ref[t * Q_TILE:(t + 1) * Q_TILE, :] = _subln(o, w_ref[...], post_scale).astype(BF16)

    _diff_attend([functools.partial(q_tile, t) for t in range(n // Q_TILE)], kcat_ref[...],
                 vcat_ref[...], lam, emit)


def _flatten_heads_kernel(k_ref, v_ref, ok_ref, ov_ref):
    shape = ok_ref.shape
    ok_ref[...] = k_ref[...].reshape(shape).astype(BF16)
    ov_ref[...] = v_ref[...].reshape(shape).astype(BF16)


def flatten_cache(cache_k, cache_v):
    b, l, past, h, e = cache_k.shape
    in_spec = pl.BlockSpec((None, None, past, h, e), lambda i, j: (i, j, 0, 0, 0))
    out_spec = pl.BlockSpec((None, None, past, h * e), lambda i, j: (i, j, 0, 0))
    sds = jax.ShapeDtypeStruct((b, l, past, h * e), BF16)
    return pl.pallas_call(
        _flatten_heads_kernel,
        grid=(b, l),
        in_specs=[in_spec, in_spec],
        out_specs=[out_spec, out_spec],
        out_shape=[sds, sds],
        compiler_params=_params(("arbitrary", "arbitrary")),
        name="flatten_cache",
    )(cache_k, cache_v)


def attn_latent(qkv, cache_k, cache_v, layer_idx, cos, sin, lam, subln_w, seq_len, post_scale):
    r = qkv.shape[0]
    b = r // seq_len
    past = cache_k.shape[2]
    kern = functools.partial(_attn_lat_kernel, post_scale=post_scale)
    blk = (seq_len, HEAD_DIM)
    cache_spec = pl.BlockSpec((None, None, past, HEAD_DIM), lambda i, h: (i, layer_idx, 0, h))
    tab_spec = pl.BlockSpec((seq_len, HEAD_DIM), lambda i, h: (0, 0))
    return pl.pallas_call(
        kern,
        grid=(b, N_HEADS),
        in_specs=[pl.BlockSpec(memory_space=pltpu.SMEM),
                  pl.BlockSpec(blk, lambda i, h: (i, h)),
                  pl.BlockSpec(blk, lambda i, h: (i, N_HEADS + h)),
                  pl.BlockSpec(blk, lambda i, h: (i, 2 * N_HEADS + h)),
                  cache_spec, cache_spec, tab_spec, tab_spec,
                  pl.BlockSpec((1, HEAD_DIM), lambda i, h: (0, 0))],
        out_specs=pl.BlockSpec(blk, lambda i, h: (i, h)),
        out_shape=jax.ShapeDtypeStruct((r, D_MODEL), BF16),
        scratch_shapes=[pltpu.VMEM((past + seq_len, HEAD_DIM), BF16),
                        pltpu.VMEM((past + seq_len, 2 * HEAD_DIM), BF16)],
        compiler_params=_params(("arbitrary", "arbitrary")),
        name="attn_latent",
    )(lam, qkv, qkv, qkv, cache_k, cache_v, cos, sin, subln_w.reshape(1, HEAD_DIM))


def _prefix_rows(x):
    row = lax.broadcasted_iota(jnp.int32, x.shape, 0)
    for j in range(N_LEVELS):
        s = 1 << j
        x = x + jnp.where(row >= s, pltpu.roll(x, s, 0), 0.0)
    return x


def _block_boundary(x, m):
    n, lanes = x.shape
    w = 2 * m
    if w >= 8:
        y = x.reshape(n // w, w, lanes)
        return jnp.broadcast_to(y[:, m - 1:m, :], y.shape).reshape(n, lanes)
    y = x.reshape(n // 8, 8, lanes)
    sub = lax.broadcasted_iota(jnp.int32, y.shape, 1)
    out = None
    for grp in range(8 // w):
        src = grp * w + m - 1
        b = jnp.broadcast_to(y[:, src:src + 1, :], y.shape)
        out = b if out is None else jnp.where(sub >= grp * w, b, out)
    return out.reshape(n, lanes)


def _scan_constants():
    t = np.arange(CHUNK)
    block = [(t[:, None] >> j) == (t[None, :] >> j) for j in range(N_LEVELS + 1)]
    fwd = [block[0]]
    for j in range(N_LEVELS):
        m = 1 << j
        up = (t & m) != 0
        fwd.append(block[j + 1] & up[:, None] & ~up[None, :])
    pair = np.stack([np.stack(fwd), np.stack([a.T for a in fwd])])
    return dict(block=jnp.asarray(np.stack(block).astype(np.float32)),
                pair=jnp.asarray(pair.astype(np.float32)))


def _gate_norm_out(o, gate, w):
    y = o * lax.rsqrt(jnp.mean(o * o, axis=-1, keepdims=True) + EPS) * w
    return (y * _silu(gate)).astype(BF16)


def _hgrn_chunk(q, z, v, lb, pair_ref, st, rev):
    e = jnp.exp(-jnp.abs(z))
    r = 1.0 / (1.0 + e)
    pos = z >= 0
    sig = jnp.where(pos, r, e * r)
    nsig = jnp.where(pos, e * r, r)
    f = lb + (1.0 - lb) * sig
    key = (1.0 - lb) * nsig
    lf = jnp.log2(f)
    incl = _prefix_rows(lf)
    base = incl - lf if rev else incl
    vb = v.astype(BF16)

    a = pair_ref[0] * _dot_nt(q.astype(BF16), key.astype(BF16))
    for lvl in range(N_LEVELS):
        dm = base - _block_boundary(incl, 1 << lvl)
        em = jnp.exp2(-jnp.abs(dm))
        a = a + pair_ref[lvl + 1] * _dot_nt((q * em).astype(BF16), (key * em).astype(BF16))

    tot = incl[CHUNK - 1:CHUNK, :]
    if rev:
        e_in = jnp.exp2(tot - base)
        e_out = jnp.exp2(base)
    else:
        e_in = jnp.exp2(incl)
        e_out = jnp.exp2(tot - incl)
    o = _dot(a.astype(BF16), vb) + _dot_nt((q * e_in).astype(BF16), st.astype(BF16))
    st_new = jnp.exp2(tot) * st + _dot_tn(vb, (key * e_out).astype(BF16))
    return o, st_new


def _hgrn_kernel(*refs, has_init, emit_state):
    refs = list(refs)
    q_ref, zf_ref, zb_ref, i_ref, g_ref, lb_ref, w_ref, pair_ref = refs[:8]
    refs = refs[8:]
    s0_ref = refs.pop(0) if has_init else None
    o_ref = refs.pop(0)
    s_out_ref = refs.pop(0) if emit_state else None
    of_ref, ob_ref, st_ref = refs

    n = q_ref.shape[0]
    nc = n // CHUNK
    lb = lb_ref[...]
    for d in range(2):
        st_ref[d] = s0_ref[d].T if has_init else jnp.zeros((HEAD_DIM, HEAD_DIM), F32)

    def body(c, carry):
        rf = pl.ds(pl.multiple_of(c * CHUNK, CHUNK), CHUNK)
        rb = pl.ds(pl.multiple_of((nc - 1 - c) * CHUNK, CHUNK), CHUNK)
        scale = HEAD_DIM ** -0.5
        load = lambda ref, rows: ref[rows, :].astype(F32)
        o, s = _hgrn_chunk(load(q_ref, rf) * scale, load(zf_ref, rf), load(i_ref, rf), lb,
                           pair_ref.at[0], st_ref[0], False)
        of_ref[rf, :] = o
        st_ref[0] = s
        o, s = _hgrn_chunk(load(q_ref, rb) * scale, load(zb_ref, rb), load(i_ref, rb), lb,
                           pair_ref.at[1], st_ref[1], True)
        ob_ref[rb, :] = o
        st_ref[1] = s
        return carry

    lax.fori_loop(0, nc, body, 0, unroll=4)
    o_ref[...] = _gate_norm_out(of_ref[...] + ob_ref[...], g_ref[...].astype(F32), w_ref[...])
    if emit_state:
        for d in range(2):
            s_out_ref[d] = st_ref[d].T


def hgrn_scan(proj, lb, norm_w, consts, seq_len, state_in, layer_idx, emit_state):
    r = proj.shape[0]
    b = r // seq_len
    has_init = state_in is not None
    blk = (seq_len, HEAD_DIM)
    col = lambda k: pl.BlockSpec(blk, lambda i, h, k=k: (i, k * N_HEADS + h))
    state_spec = pl.BlockSpec((None, None, 2, None, HEAD_DIM, HEAD_DIM),
                              lambda i, h: (i, layer_idx, 0, h, 0, 0))
    in_specs = [col(0), col(1), col(2), col(3), col(4),
                pl.BlockSpec((None, 1, HEAD_DIM), lambda i, h: (h, 0, 0)),
                pl.BlockSpec((1, HEAD_DIM), lambda i, h: (0, 0)),
                pl.BlockSpec((2, N_LEVELS + 1, CHUNK, CHUNK), lambda i, h: (0, 0, 0, 0))]
    args = [proj, proj, proj, proj, proj, lb.reshape(N_HEADS, 1, HEAD_DIM),
            norm_w.reshape(1, HEAD_DIM), consts['pair']]
    if has_init:
        in_specs.append(state_spec)
        args.append(state_in)
    out_specs = [pl.BlockSpec(blk, lambda i, h: (i, h))]
    out_shape = [jax.ShapeDtypeStruct((r, D_MODEL), BF16)]
    if emit_state:
        out_specs.append(pl.BlockSpec((None, None, 2, None, HEAD_DIM, HEAD_DIM),
                                      lambda i, h: (i, 0, 0, h, 0, 0)))
        out_shape.append(jax.ShapeDtypeStruct((b, 1, 2, N_HEADS, HEAD_DIM, HEAD_DIM), F32))
    kern = functools.partial(_hgrn_kernel, has_init=has_init, emit_state=emit_state)
    res = pl.pallas_call(
        kern,
        grid=(b, N_HEADS),
        in_specs=in_specs,
        out_specs=out_specs,
        out_shape=out_shape,
        scratch_shapes=[pltpu.VMEM(blk, F32), pltpu.VMEM(blk, F32),
                        pltpu.VMEM((2, HEAD_DIM, HEAD_DIM), F32)],
        compiler_params=_params(("arbitrary", "arbitrary")),
        name="hgrn_scan",
    )(*args)
    return res if emit_state else (res[0], None)


def _lane_column(x, lane_idx):
    lane = lax.broadcasted_iota(jnp.int32, x.shape, 1)
    colv = jnp.sum(jnp.where(lane == lane_idx, x, 0.0), axis=1, keepdims=True)
    return jnp.broadcast_to(colv, x.shape)


def _gdn_prepare(items, block_ref):
    row = lax.broadcasted_iota(jnp.int32, (CHUNK, CHUNK), 0)
    colm = lax.broadcasted_iota(jnp.int32, (CHUNK, CHUNK), 1)
    pre = []
    for q, k, v, g, beta, kk, qk, rev in items:
        g_end = g[0:1, :] if rev else g[CHUNK - 1:CHUNK, :]
        within = (row <= colm) if rev else (row >= colm)
        strict = (row < colm) if rev else (row > colm)
        decay = jnp.exp2(jnp.where(within, g - g.T, -jnp.inf))
        a = jnp.where(strict, beta * kk * decay, 0.0)
        eg = jnp.exp2(g)
        x = jnp.concatenate([v * beta, k * (beta * eg)], axis=1).astype(BF16)
        p = (qk * decay).astype(BF16)
        ke = (k * jnp.exp2(g_end - g)).astype(BF16)
        pre.append((a, x, p, ke, q * eg, jnp.exp2(g_end)))
    ts = [block_ref[0] - it[0] * (block_ref[1] - block_ref[0]) for it in pre]
    for lvl in range(1, N_LEVELS):
        sel = block_ref[lvl + 1] - block_ref[lvl]
        tbs = [t.astype(BF16) for t in ts]
        lts = [_dot((it[0] * sel).astype(BF16), tb).astype(BF16) for it, tb in zip(pre, tbs)]
        ts = [t - _dot(tb, lt) for t, tb, lt in zip(ts, tbs, lts)]
    wus = [_dot(t.astype(BF16), it[1]).astype(BF16) for t, it in zip(ts, pre)]
    pwus = [_dot(it[2], wu) for it, wu in zip(pre, wus)]
    kwus = [_dot_tn(it[3], wu) for it, wu in zip(pre, wus)]
    out = []
    for it, pwu, kwu in zip(pre, pwus, kwus):
        qs = (it[4] - pwu[:, HEAD_DIM:]).astype(BF16)
        out.append((qs, pwu[:, :HEAD_DIM], kwu[:, HEAD_DIM:].astype(BF16), kwu[:, :HEAD_DIM], it[5]))
    return out


def _gdn_kernel(*refs, has_init, emit_state, hp):
    refs = list(refs)
    (q_ref, k_ref, v_ref, gate_ref, ab_ref, cq_ref, ck_ref, cv_ref, rate_ref, bias_ref,
     w_ref, block_ref) = refs[:12]
    refs = refs[12:]
    s0_ref = refs.pop(0) if has_init else None
    o_ref = refs.pop(0)
    s_out_ref = refs.pop(0) if emit_state else None
    qn_ref, kn_ref, vn_ref, qs_ref, ms_ref, o_acc_ref, b_ref, dec_ref, st_ref = refs

    n = q_ref.shape[0]
    nc = n // CHUNK
    group = max(1, min(nc, GDN_CHAINS // (2 * hp)))
    head0 = pl.program_id(1) * hp
    pos = lax.broadcasted_iota(jnp.int32, (n, hp * HEAD_DIM), 0)

    def conv_silu(x_ref, cw_ref):
        x = x_ref[...].astype(F32)
        prev = jnp.where(pos == 0, 0.0, pltpu.roll(x, 1, 0))
        nxt = jnp.where(pos == n - 1, 0.0, pltpu.roll(x, n - 1, 0))
        return _silu(prev * cw_ref[0:1, :] + x * cw_ref[1:2, :] + nxt * cw_ref[2:3, :])

    def l2norm(x):
        return x * lax.rsqrt(jnp.sum(x * x, axis=-1, keepdims=True) + 1e-6)

    qc = conv_silu(q_ref, cq_ref)
    kc = conv_silu(k_ref, ck_ref)
    vn_ref[...] = conv_silu(v_ref, cv_ref)
    for hh in range(hp):
        sl = slice(hh * HEAD_DIM, (hh + 1) * HEAD_DIM)
        qn_ref[:, sl] = l2norm(qc[:, sl]) * (HEAD_DIM ** -0.5)
        kn_ref[:, sl] = l2norm(kc[:, sl])
        for d in range(2):
            st_ref[d, hh] = s0_ref[d, hh] if has_init else jnp.zeros((HEAD_DIM, HEAD_DIM), F32)

    def chunk_rows(c):
        return pl.ds(pl.multiple_of(c * CHUNK, CHUNK), CHUNK)

    def prepare_body(i, carry):
        items, dest = [], []
        lane = lax.broadcasted_iota(jnp.int32, (CHUNK, HEAD_DIM), 1)
        for grp in range(group):
            rows = chunk_rows(i * group + grp)
            ab = ab_ref[rows, :]
            la = -(rate_ref[...] * LOG2E) * jax.nn.softplus(ab + bias_ref[...])
            sg_all = jax.nn.sigmoid(ab)
            incl = _prefix_rows(la)
            tot = incl[CHUNK - 1:CHUNK, :]
            g_all = jnp.where(lane >= N_HEADS, tot - incl + la, incl)
            for hh in range(hp):
                sl = slice(hh * HEAD_DIM, (hh + 1) * HEAD_DIM)
                q = qn_ref[rows, sl]
                k = kn_ref[rows, sl]
                v = vn_ref[rows, sl]
                kb = k.astype(BF16)
                kk = _dot_nt(kb, kb)
                qk = _dot_nt(q.astype(BF16), kb)
                for d in range(2):
                    g = _lane_column(g_all, d * N_HEADS + head0 + hh)
                    beta = _lane_column(sg_all, (2 + d) * N_HEADS + head0 + hh)
                    items.append((q, k, v, g, beta, kk, qk, d == 1))
                    dest.append((d, hh, rows))
        for (d, hh, rows), (qs, o0, ms, b, dec) in zip(dest, _gdn_prepare(items, block_ref)):
            qs_ref[d, hh, rows, :] = qs
            ms_ref[d, hh, rows, :] = ms
            o_acc_ref[d, hh, rows, :] = o0
            b_ref[d, hh, rows, :] = b
            dec_ref[d, hh, rows, :] = jnp.broadcast_to(dec, (CHUNK, HEAD_DIM))
        return carry

    lax.fori_loop(0, nc // group, prepare_body, 0)

    def scan_body(c, carry):
        for hh in range(hp):
            for d, cc in ((0, c), (1, nc - 1 - c)):
                rows = chunk_rows(cc)
                s = st_ref[d, hh]
                sb = s.astype(BF16)
                o_acc_ref[d, hh, rows, :] += _dot(qs_ref[d, hh, rows, :], sb)
                st_ref[d, hh] = (dec_ref[d, hh, rows, :] * s - _dot(ms_ref[d, hh, rows, :], sb)
                                 + b_ref[d, hh, rows, :])
        return carry

    lax.fori_loop(0, nc, scan_body, 0)
    for hh in range(hp):
        sl = slice(hh * HEAD_DIM, (hh + 1) * HEAD_DIM)
        o_ref[:, sl] = _gate_norm_out(o_acc_ref[0, hh] + o_acc_ref[1, hh],
                                      gate_ref[:, sl].astype(F32), w_ref[...])
    if emit_state:
        for d in range(2):
            for hh in range(hp):
                s_out_ref[d, hh] = st_ref[d, hh]


def gdn_scan(proj, ab, conv_w, rate, bias, norm_w, consts, seq_len, state_in, layer_idx,
             emit_state):
    r = proj.shape[0]
    b = r // seq_len
    has_init = state_in is not None
    nc = seq_len // CHUNK
    hp = GDN_HEADS_PER_STEP
    blk = (seq_len, hp * HEAD_DIM)
    nhb = N_HEADS // hp
    col = lambda k: pl.BlockSpec(blk, lambda i, h, k=k: (i, k * nhb + h))
    ccol = lambda k: pl.BlockSpec((3, hp * HEAD_DIM), lambda i, h, k=k: (0, k * nhb + h))
    vec = pl.BlockSpec((1, HEAD_DIM), lambda i, h: (0, 0))
    state_blk = (None, None, 2, hp, HEAD_DIM, HEAD_DIM)
    in_specs = [col(0), col(1), col(2), col(3),
                pl.BlockSpec((seq_len, HEAD_DIM), lambda i, h: (i, 0)),
                ccol(0), ccol(1), ccol(2), vec, vec, vec,
                pl.BlockSpec((N_LEVELS + 1, CHUNK, CHUNK), lambda i, h: (0, 0, 0))]
    args = [proj, proj, proj, proj, ab, conv_w, conv_w, conv_w, rate, bias,
            norm_w.reshape(1, HEAD_DIM), consts['block']]
    if has_init:
        in_specs.append(pl.BlockSpec(state_blk, lambda i, h: (i, layer_idx, 0, h, 0, 0)))
        args.append(state_in)
    out_specs = [pl.BlockSpec(blk, lambda i, h: (i, h))]
    out_shape = [jax.ShapeDtypeStruct((r, D_MODEL), BF16)]
    if emit_state:
        out_specs.append(pl.BlockSpec(state_blk, lambda i, h: (i, 0, 0, h, 0, 0)))
        out_shape.append(jax.ShapeDtypeStruct((b, 1, 2, N_HEADS, HEAD_DIM, HEAD_DIM), F32))
    kern = functools.partial(_gdn_kernel, has_init=has_init, emit_state=emit_state, hp=hp)
    per_head = (2, hp, seq_len, HEAD_DIM)
    res = pl.pallas_call(
        kern,
        grid=(b, nhb),
        in_specs=in_specs,
        out_specs=out_specs,
        out_shape=out_shape,
        scratch_shapes=[pltpu.VMEM(blk, F32)] * 3
        + [pltpu.VMEM(per_head, BF16)] * 2 + [pltpu.VMEM(per_head, F32)] * 3
        + [pltpu.VMEM((2, hp, HEAD_DIM, HEAD_DIM), F32)],
        compiler_params=_params(("arbitrary", "arbitrary")),
        name="gdn_scan",
    )(*args)
    return res if emit_state else (res[0], None)


def _rope_tables(n):
    rows = n // GRID_W
    row = jnp.repeat(jnp.arange(rows, dtype=F32), GRID_W)
    colp = jnp.tile(jnp.arange(GRID_W, dtype=F32), rows)
    nf = DA_HD // 4
    inv = ROPE_BASE ** (-jnp.arange(nf, dtype=F32) / nf)
    ar = row[:, None] * inv
    ac = colp[:, None] * inv
    cr, sr, cc, sc = jnp.cos(ar), jnp.sin(ar), jnp.cos(ac), jnp.sin(ac)
    cos = jnp.concatenate([cr, cr, cc, cc] * 2, axis=1)
    sin = jnp.concatenate([-sr, sr, -sc, sc] * 2, axis=1)
    return cos, sin


def _pad_halves(a, width, padded):
    halves = a.reshape(a.shape[:-1] + (2, width))
    halves = jnp.pad(halves, [(0, 0)] * (halves.ndim - 1) + [(0, padded - width)])
    return halves.reshape(a.shape[:-1] + (2 * padded,))


def _trunk(x, mods, rows_per_mod, seq_len, P, cache):
    ctx_mode = cache is None
    depth = P['w_mod'].shape[0]
    n_attn = P['attn_w_in'].shape[0]
    kv, hgrn_state, gdn_state = None, None, None
    for i in range(depth):
        sh1, sc1, g1, sh2, sc2, g2 = mods[i]
        kind, j = i % 3, i // 3
        proj = functools.partial(norm_proj, x, P['norm_g'][i, 0], sc1, sh1,
                                 rows_per_mod=rows_per_mod)
        if kind == 0:
            qkv = proj(P['attn_w_in'][j], tn=1024, out_dtype=BF16)
            lam_init = 0.8 - 0.6 * math.exp(-0.3 * i)
            lp = P['attn_lambda'][j]
            lam = (jnp.exp(jnp.sum(lp[0] * lp[1])) - jnp.exp(jnp.sum(lp[2] * lp[3]))
                   + lam_init).reshape(1)
            if ctx_mode:
                o, kv = attn_context(qkv, lam, P['attn_subln'][j], seq_len, 1.0 - lam_init, j,
                                     n_attn, kv)
            else:
                o = attn_latent(qkv, cache['attn_k'], cache['attn_v'], j, cache['cos'],
                                cache['sin'], lam, P['attn_subln'][j], seq_len, 1.0 - lam_init)
            w_out = P['attn_w_out'][j]
        elif kind == 1:
            o, st = hgrn_scan(proj(P['hgrn_w_in'][j], tn=1024, out_dtype=BF16), P['lb_all'][i],
                              P['hgrn_norm'][j], P['consts'], seq_len,
                              None if ctx_mode else cache['hgrn'], j, ctx_mode)
            if ctx_mode:
                hgrn_state = st
            w_out = P['hgrn_w_out'][j]
        else:
            o, st = gdn_scan(proj(P['gdn_w_main'][j], tn=1024, out_dtype=BF16),
                             proj(P['gdn_w_tail'][j], tn=HEAD_DIM, out_dtype=F32),
                             P['gdn_conv'][j], P['gdn_rate'][j], P['gdn_bias'][j],
                             P['gdn_norm'][j], P['consts'], seq_len,
                             None if ctx_mode else cache['gdn'], j, ctx_mode)
            if ctx_mode:
                gdn_state = st
            w_out = P['gdn_w_out'][j]
        x = out_proj_residual(o, w_out, x, g1, rows_per_mod)
        x = conv_ffn_residual(x, P['norm_g'][i, 1], sc2, sh2, g2, P['ffn_w_up'][i],
                              P['ffn_conv'][i], P['ffn_conv_b'][i], P['ffn_w_down'][i],
                              P['final_g'], rows_per_mod, seq_len, i == depth - 1)
    return x, (kv, hgrn_state, gdn_state)


def kernel(x_prompt, x_sample, cache_attn_k, cache_attn_v, state_hgrn, state_gdn, c, c_ctx, norm_g, w_mod, b_mod, final_g, attn_w_in, attn_lambda, attn_subln, attn_w_out, hgrn_w_in, hgrn_lb, hgrn_norm, hgrn_w_out, gdn_w_in, gdn_conv, gdn_a_log, gdn_dt_bias, gdn_norm, gdn_w_out, ffn_w_up, ffn_conv, ffn_conv_b, ffn_w_down):
    batch, seq, d = x_prompt.shape
    dec_batch, dec_seq, _ = x_sample.shape
    depth = w_mod.shape[0]
    n_gdn = gdn_w_in.shape[0]

    lb_all = jnp.cumsum(jax.nn.softmax(hgrn_lb, axis=0), axis=0)
    lb_all = lb_all - lb_all[0]

    d_ff = ffn_w_down.shape[1]
    step_cols = FF_TILE * FFN_TILES_PER_STEP
    d_ff_pad = -(-d_ff // step_cols) * step_cols

    pad = jnp.zeros((n_gdn, HEAD_DIM - 2 * N_HEADS), F32)
    gdn_rate = jnp.concatenate([jnp.exp(gdn_a_log).reshape(n_gdn, 2 * N_HEADS), pad], axis=1)
    gdn_bias = jnp.concatenate([gdn_dt_bias.reshape(n_gdn, 2 * N_HEADS), pad], axis=1)
    tail = gdn_w_in[:, :, 4 * d:]
    tail = jnp.concatenate([tail, jnp.zeros((n_gdn, d, HEAD_DIM - tail.shape[2]), F32)], axis=2)

    P = dict(norm_g=norm_g, w_mod=w_mod, final_g=final_g,
             attn_w_in=attn_w_in.astype(BF16), attn_lambda=attn_lambda, attn_subln=attn_subln,
             attn_w_out=attn_w_out.astype(BF16),
             hgrn_w_in=hgrn_w_in.astype(BF16), lb_all=lb_all, hgrn_norm=hgrn_norm,
             hgrn_w_out=hgrn_w_out.astype(BF16),
             gdn_w_main=gdn_w_in[:, :, :4 * d].astype(BF16), gdn_w_tail=tail.astype(BF16),
             gdn_conv=gdn_conv, gdn_rate=gdn_rate.reshape(n_gdn, 1, HEAD_DIM),
             gdn_bias=gdn_bias.reshape(n_gdn, 1, HEAD_DIM), gdn_norm=gdn_norm,
             gdn_w_out=gdn_w_out.astype(BF16),
             ffn_w_up=_pad_halves(ffn_w_up, d_ff, d_ff_pad).astype(BF16),
             ffn_conv=_pad_halves(ffn_conv, d_ff, d_ff_pad),
             ffn_conv_b=_pad_halves(ffn_conv_b, d_ff, d_ff_pad),
             ffn_w_down=jnp.pad(ffn_w_down, ((0, 0), (0, d_ff_pad - d_ff), (0, 0))).astype(BF16),
             consts=_scan_constants())

    cond = jnp.zeros((16, d), F32).at[:dec_batch].set(c).at[dec_batch].set(c_ctx)
    mod = modulation_all(cond, w_mod, b_mod).reshape(depth, 16, 6, 1, d)
    mods_lat = [[mod[i, :dec_batch, t] for t in range(6)] for i in range(depth)]
    mods_ctx = [[mod[i, dec_batch:dec_batch + 1, t] for t in range(6)] for i in range(depth)]

    y_prompt, ((new_k, new_v), new_hgrn, new_gdn) = _trunk(
        x_prompt.reshape(batch * seq, d), mods_ctx, batch * seq, seq, P, None)

    cos, sin = _rope_tables(dec_seq)
    flat_k, flat_v = flatten_cache(cache_attn_k, cache_attn_v)
    cache = dict(attn_k=flat_k, attn_v=flat_v, hgrn=state_hgrn, gdn=state_gdn, cos=cos, sin=sin)
    y_sample, _ = _trunk(x_sample.reshape(dec_batch * dec_seq, d), mods_lat, dec_seq, dec_seq, P,
                         cache)

    return (y_prompt.reshape(batch, seq, d), y_sample.reshape(dec_batch, dec_seq, d),
            new_k, new_v, new_hgrn, new_gdn)
```

```python
import functools
import math

import jax
import jax.numpy as jnp
import numpy as np
from jax import lax
from jax.experimental import pallas as pl
from jax.experimental.pallas import tpu as pltpu

F32 = jnp.float32
BF16 = jnp.bfloat16

D_MODEL = 1024
N_HEADS = 8
HEAD_DIM = 128
DA_HD = 64
GRID_W = 64
ROPE_BASE = 10000.0
D_FF = 2816
EPS = 1e-6
LOG2E = 1.4426950408889634
CHUNK = 128
N_LEVELS = 7
ROW_TILE = 1024
FF_TILE = 256
FFN_TILES_PER_STEP = 6
Q_TILE = 256
SCORE_LOOKAHEAD = 8
HGRN_UNROLL = 4
GDN_CHAINS = 16
GDN_HEADS_PER_STEP = 4
VMEM_LIMIT = 56 * 1024 * 1024


def _params(sem, vmem=VMEM_LIMIT):
    return pltpu.CompilerParams(dimension_semantics=sem, vmem_limit_bytes=vmem)


def _dot(a, b):
    return jnp.dot(a, b, preferred_element_type=F32)


def _dot_nt(a, b):
    return lax.dot_general(a, b, (((1,), (1,)), ((), ())), preferred_element_type=F32)


def _dot_tn(a, b):
    return lax.dot_general(a, b, (((0,), (0,)), ((), ())), preferred_element_type=F32)


def _silu(x):
    return x * jax.nn.sigmoid(x)


def _norm_mod(x, g, sc, sh):
    y = x * lax.rsqrt(jnp.mean(x * x, axis=-1, keepdims=True) + EPS)
    return (y * g) * (1.0 + sc) + sh


def _mod_kernel(c_ref, w_ref, b_ref, o_ref):
    s = _silu(c_ref[...]).astype(BF16)
    o_ref[...] = _dot(s, w_ref[...].astype(BF16)) + b_ref[...]


def modulation_all(cond, w_mod, b_mod):
    depth, d, n = w_mod.shape
    tn = 1024
    return pl.pallas_call(
        _mod_kernel,
        grid=(depth, n // tn),
        in_specs=[pl.BlockSpec((16, d), lambda l, j: (0, 0)),
                  pl.BlockSpec((None, d, tn), lambda l, j: (l, 0, j)),
                  pl.BlockSpec((None, 1, tn), lambda l, j: (l, 0, j))],
        out_specs=pl.BlockSpec((None, 16, tn), lambda l, j: (l, 0, j)),
        out_shape=jax.ShapeDtypeStruct((depth, 16, n), F32),
        compiler_params=_params(("arbitrary", "arbitrary")),
        name="modulation",
    )(cond, w_mod, b_mod.reshape(depth, 1, n))


def _proj_kernel(x_ref, g_ref, sc_ref, sh_ref, w_ref, o_ref, h_ref):
    @pl.when(pl.program_id(1) == 0)
    def _():
        h_ref[...] = _norm_mod(x_ref[...], g_ref[...], sc_ref[...], sh_ref[...]).astype(BF16)

    o_ref[...] = _dot(h_ref[...], w_ref[...]).astype(o_ref.dtype)


def norm_proj(x, g, sc, sh, w, rows_per_mod, tn, out_dtype):
    r, d = x.shape
    n = w.shape[1]
    tm = ROW_TILE
    mod_spec = pl.BlockSpec((None, 1, d), lambda i, j: ((i * tm) // rows_per_mod, 0, 0))
    return pl.pallas_call(
        _proj_kernel,
        grid=(r // tm, n // tn),
        in_specs=[pl.BlockSpec((tm, d), lambda i, j: (i, 0)),
                  pl.BlockSpec((1, d), lambda i, j: (0, 0)),
                  mod_spec, mod_spec,
                  pl.BlockSpec((d, tn), lambda i, j: (0, j))],
        out_specs=pl.BlockSpec((tm, tn), lambda i, j: (i, j)),
        out_shape=jax.ShapeDtypeStruct((r, n), out_dtype),
        scratch_shapes=[pltpu.VMEM((tm, d), BF16)],
        compiler_params=_params(("arbitrary", "arbitrary")),
        name="norm_proj",
    )(x, g.reshape(1, d), sc, sh, w)


def _out_proj_kernel(o_ref, w_ref, x_ref, gate_ref, y_ref):
    y_ref[...] = x_ref[...] + gate_ref[...] * _dot(o_ref[...], w_ref[...])


def out_proj_residual(o, w, x, gate, rows_per_mod):
    r, d = x.shape
    tm = ROW_TILE
    return pl.pallas_call(
        _out_proj_kernel,
        grid=(r // tm,),
        in_specs=[pl.BlockSpec((tm, d), lambda i: (i, 0)),
                  pl.BlockSpec((d, d), lambda i: (0, 0)),
                  pl.BlockSpec((tm, d), lambda i: (i, 0)),
                  pl.BlockSpec((None, 1, d), lambda i: ((i * tm) // rows_per_mod, 0, 0))],
        out_specs=pl.BlockSpec((tm, d), lambda i: (i, 0)),
        out_shape=jax.ShapeDtypeStruct((r, d), F32),
        compiler_params=_params(("arbitrary",)),
        name="out_proj",
    )(o, w, x, gate)


def _ffn_kernel(*refs, seq_len, final_norm):
    x_ref, g_ref, sc_ref, sh_ref, gate_ref = refs[:5]
    nt = FFN_TILES_PER_STEP
    wv_refs, wg_refs, cv_refs, cg_refs, bv_refs, bg_refs, wd_refs = (
        refs[5 + k * nt:5 + (k + 1) * nt] for k in range(7))
    fg_ref, y_ref, h_ref, acc_ref = refs[5 + 7 * nt:]
    j = pl.program_id(1)

    @pl.when(j == 0)
    def _():
        h_ref[...] = _norm_mod(x_ref[...], g_ref[...], sc_ref[...], sh_ref[...]).astype(BF16)
        acc_ref[...] = jnp.zeros_like(acc_ref)

    tm = h_ref.shape[0]
    h = h_ref[...]
    pos = lax.broadcasted_iota(jnp.int32, (tm, FF_TILE), 0) & (seq_len - 1)
    first = pos == 0
    last = pos == seq_len - 1

    def conv(u, cw_ref, b_ref):
        prev = jnp.where(first, 0.0, pltpu.roll(u, 1, 0))
        nxt = jnp.where(last, 0.0, pltpu.roll(u, tm - 1, 0))
        return prev * cw_ref[0:1, :] + u * cw_ref[1:2, :] + nxt * cw_ref[2:3, :] + b_ref[...]

    ups = [(_dot(h, wv_refs[t][...]), _dot(h, wg_refs[t][...])) for t in range(nt)]
    for t in range(nt):
        val = conv(ups[t][0], cv_refs[t], bv_refs[t])
        gte = conv(ups[t][1], cg_refs[t], bg_refs[t])
        act = (val * _silu(gte)).astype(BF16)
        acc_ref[...] += _dot(act, wd_refs[t][...])

    @pl.when(j == pl.num_programs(1) - 1)
    def _():
        y = x_ref[...] + gate_ref[...] * acc_ref[...]
        if final_norm:
            y = y * lax.rsqrt(jnp.mean(y * y, axis=-1, keepdims=True) + EPS) * fg_ref[...]
        y_ref[...] = y


def conv_ffn_residual(x, g, sc, sh, gate, w_up, conv_w, conv_b, w_down, final_g, rows_per_mod,
                      seq_len, final_norm):
    r, d = x.shape
    tm = ROW_TILE
    nt = FFN_TILES_PER_STEP
    nf = w_down.shape[0] // FF_TILE
    mod_spec = pl.BlockSpec((None, 1, d), lambda i, j: ((i * tm) // rows_per_mod, 0, 0))
    kern = functools.partial(_ffn_kernel, seq_len=seq_len, final_norm=final_norm)
    conv_b = conv_b.reshape(1, -1)

    def cols(rows, half):
        return [pl.BlockSpec((rows, FF_TILE), lambda i, j, t=t: (0, half * nf + nt * j + t))
                for t in range(nt)]

    in_specs = ([pl.BlockSpec((tm, d), lambda i, j: (i, 0)),
                 pl.BlockSpec((1, d), lambda i, j: (0, 0)),
                 mod_spec, mod_spec, mod_spec]
                + cols(d, 0) + cols(d, 1) + cols(3, 0) + cols(3, 1) + cols(1, 0) + cols(1, 1)
                + [pl.BlockSpec((FF_TILE, d), lambda i, j, t=t: (nt * j + t, 0)) for t in range(nt)]
                + [pl.BlockSpec((1, d), lambda i, j: (0, 0))])
    args = ([x, g.reshape(1, d), sc, sh, gate] + [w_up] * (2 * nt) + [conv_w] * (2 * nt)
            + [conv_b] * (2 * nt) + [w_down] * nt + [final_g.reshape(1, d)])
    return pl.pallas_call(
        kern,
        grid=(r // tm, nf // nt),
        in_specs=in_specs,
        out_specs=pl.BlockSpec((tm, d), lambda i, j: (i, 0)),
        out_shape=jax.ShapeDtypeStruct((r, d), F32),
        scratch_shapes=[pltpu.VMEM((tm, d), BF16), pltpu.VMEM((tm, d), F32)],
        compiler_params=_params(("arbitrary", "arbitrary")),
        name="conv_ffn",
    )(*args)


def _ones_column_block(rows):
    lane = lax.broadcasted_iota(jnp.int32, (rows, HEAD_DIM), 1)
    return jnp.where(lane == 0, 1.0, 0.0).astype(BF16)


def _diff_attend(tiles, lam, emit):
    chains = [(t, c) for t in range(len(tiles)) for c in range(2)]
    q_cache = {}

    def scores(t, c):
        if t not in q_cache:
            q_cache[t] = tiles[t][0]()
        q = q_cache[t]
        lane = lax.broadcasted_iota(jnp.int32, q.shape, 1)
        qc = jnp.where((lane < DA_HD) == (c == 0), q, 0.0).astype(BF16)
        return _dot_nt(qc, tiles[t][1])

    ahead = SCORE_LOOKAHEAD
    pending = [scores(*ch) for ch in chains[:ahead]]
    first = None
    for idx, (t, c) in enumerate(chains):
        if idx + ahead < len(chains):
            pending.append(scores(*chains[idx + ahead]))
        s = pending.pop(0)
        e = jnp.exp2(s - jnp.max(s, axis=-1, keepdims=True)).astype(BF16)
        pv = _dot(e, tiles[t][2])
        o = pv[:, :HEAD_DIM] / pv[:, HEAD_DIM:HEAD_DIM + 1]
        if c == 0:
            first = o
        else:
            emit(t, first - lam * o)


def _subln(o, w, post_scale):
    return o * lax.rsqrt(jnp.mean(o * o, axis=-1, keepdims=True) + 1e-5) * w * post_scale


def _attn_ctx_kernel(*refs, post_scale, layer_idx, first_layer):
    if first_layer:
        lam_ref, q_ref, k_ref, v_ref, w_ref, o_ref, nk_ref, nv_ref = refs
    else:
        lam_ref, q_ref, k_ref, v_ref, w_ref, _, _, o_ref, nk_ref, nv_ref = refs
    lam = lam_ref[0]
    n = q_ref.shape[0]
    ones = _ones_column_block(n)
    scale = DA_HD ** -0.5 * LOG2E
    for h in range(N_HEADS):
        sl = slice(h * HEAD_DIM, (h + 1) * HEAD_DIM)
        v1 = jnp.concatenate([v_ref[:, sl], ones], axis=1)

        def emit(t, o, sl=sl):
            o_ref[:, sl] = _subln(o, w_ref[...], post_scale).astype(BF16)

        _diff_attend([(lambda sl=sl: q_ref[:, sl].astype(F32) * scale, k_ref[:, sl], v1)], lam,
                     emit)

    kv_shape = (n, N_HEADS, HEAD_DIM)
    if first_layer:
        for l in range(nk_ref.shape[0]):
            if l == layer_idx:
                nk_ref[l] = k_ref[...].astype(F32).reshape(kv_shape)
                nv_ref[l] = v_ref[...].astype(F32).reshape(kv_shape)
            else:
                nk_ref[l] = jnp.zeros(kv_shape, F32)
                nv_ref[l] = jnp.zeros(kv_shape, F32)
    else:
        nk_ref[...] = k_ref[...].astype(F32).reshape(kv_shape)
        nv_ref[...] = v_ref[...].astype(F32).reshape(kv_shape)


def attn_context(qkv, lam, subln_w, seq_len, post_scale, layer_idx, n_layers, kv_prev):
    r = qkv.shape[0]
    b = r // seq_len
    d = D_MODEL
    first_layer = kv_prev is None
    kern = functools.partial(_attn_ctx_kernel, post_scale=post_scale, layer_idx=layer_idx,
                             first_layer=first_layer)
    in_specs = [pl.BlockSpec(memory_space=pltpu.SMEM),
                pl.BlockSpec((seq_len, d), lambda i: (i, 0)),
                pl.BlockSpec((seq_len, d), lambda i: (i, 1)),
                pl.BlockSpec((seq_len, d), lambda i: (i, 2)),
                pl.BlockSpec((1, HEAD_DIM), lambda i: (0, 0))]
    args = [lam, qkv, qkv, qkv, subln_w.reshape(1, HEAD_DIM)]
    kv_sds = jax.ShapeDtypeStruct((b, n_layers, seq_len, N_HEADS, HEAD_DIM), F32)
    if first_layer:
        kv_spec = pl.BlockSpec((None, n_layers, seq_len, N_HEADS, HEAD_DIM),
                               lambda i: (i, 0, 0, 0, 0))
        aliases = {}
    else:
        kv_spec = pl.BlockSpec((None, None, seq_len, N_HEADS, HEAD_DIM),
                               lambda i: (i, layer_idx, 0, 0, 0))
        in_specs += [pl.BlockSpec(memory_space=pl.ANY)] * 2
        args += list(kv_prev)
        aliases = {5: 1, 6: 2}
    o, nk, nv = pl.pallas_call(
        kern,
        grid=(b,),
        in_specs=in_specs,
        out_specs=[pl.BlockSpec((seq_len, d), lambda i: (i, 0)), kv_spec, kv_spec],
        out_shape=[jax.ShapeDtypeStruct((r, d), BF16), kv_sds, kv_sds],
        input_output_aliases=aliases,
        compiler_params=_params(("arbitrary",)),
        name="attn_context",
    )(*args)
    return o, (nk, nv)


def _rope(x, cos, sin):
    lane = lax.broadcasted_iota(jnp.int32, x.shape, 1)
    partner = jnp.where((lane & 16) == 0, pltpu.roll(x, HEAD_DIM - 16, 1), pltpu.roll(x, 16, 1))
    return x * cos + partner * sin


def _attn_lat_kernel(lam_ref, q_ref, k_ref, v_ref, ck_ref, cv_ref, cos_ref, sin_ref, w_ref, o_ref,
                     kcat_ref, vcat_ref, *, post_scale):
    lam = lam_ref[0]
    past = ck_ref.shape[0]
    n = q_ref.shape[0]
    kcat_ref[0:past, :] = ck_ref[...]
    kcat_ref[past:past + n, :] = _rope(k_ref[...].astype(F32), cos_ref[...],
                                       sin_ref[...]).astype(BF16)
    vcat_ref[0:past, 0:HEAD_DIM] = cv_ref[...]
    vcat_ref[past:past + n, 0:HEAD_DIM] = v_ref[...]
    vcat_ref[:, HEAD_DIM:] = _ones_column_block(past + n)
    scale = DA_HD ** -0.5 * LOG2E

    def q_tile(t):
        rows = slice(t * Q_TILE, (t + 1) * Q_TILE)
        return _rope(q_ref[rows, :].astype(F32), cos_ref[rows, :], sin_ref[rows, :]) * scale

    def emit(t, o):
        o_ref[t * Q_TILE:(t + 1) * Q_TILE, :] = _subln(o, w_ref[...], post_scale).astype(BF16)

    k = kcat_ref[...]
    v1 = vcat_ref[...]
    _diff_attend([(functools.partial(q_tile, t), k, v1) for t in range(n // Q_TILE)], lam, emit)


def _flatten_heads_kernel(k_ref, v_ref, ok_ref, ov_ref):
    shape = ok_ref.shape
    ok_ref[...] = k_ref[...].reshape(shape).astype(BF16)
    ov_ref[...] = v_ref[...].reshape(shape).astype(BF16)


def flatten_cache(cache_k, cache_v):
    b, l, past, h, e = cache_k.shape
    in_spec = pl.BlockSpec((None, None, past, h, e), lambda i, j: (i, j, 0, 0, 0))
    out_spec = pl.BlockSpec((None, None, past, h * e), lambda i, j: (i, j, 0, 0))
    sds = jax.ShapeDtypeStruct((b, l, past, h * e), BF16)
    return pl.pallas_call(
        _flatten_heads_kernel,
        grid=(b, l),
        in_specs=[in_spec, in_spec],
        out_specs=[out_spec, out_spec],
        out_shape=[sds, sds],
        compiler_params=_params(("arbitrary", "arbitrary")),
        name="flatten_cache",
    )(cache_k, cache_v)


def attn_latent(qkv, cache_k, cache_v, layer_idx, cos, sin, lam, subln_w, seq_len, post_scale):
    r = qkv.shape[0]
    b = r // seq_len
    past = cache_k.shape[2]
    kern = functools.partial(_attn_lat_kernel, post_scale=post_scale)
    blk = (seq_len, HEAD_DIM)
    cache_spec = pl.BlockSpec((None, None, past, HEAD_DIM), lambda i, h: (i, layer_idx, 0, h))
    tab_spec = pl.BlockSpec((seq_len, HEAD_DIM), lambda i, h: (0, 0))
    return pl.pallas_call(
        kern,
        grid=(b, N_HEADS),
        in_specs=[pl.BlockSpec(memory_space=pltpu.SMEM),
                  pl.BlockSpec(blk, lambda i, h: (i, h)),
                  pl.BlockSpec(blk, lambda i, h: (i, N_HEADS + h)),
                  pl.BlockSpec(blk, lambda i, h: (i, 2 * N_HEADS + h)),
                  cache_spec, cache_spec, tab_spec, tab_spec,
                  pl.BlockSpec((1, HEAD_DIM), lambda i, h: (0, 0))],
        out_specs=pl.BlockSpec(blk, lambda i, h: (i, h)),
        out_shape=jax.ShapeDtypeStruct((r, D_MODEL), BF16),
        scratch_shapes=[pltpu.VMEM((past + seq_len, HEAD_DIM), BF16),
                        pltpu.VMEM((past + seq_len, 2 * HEAD_DIM), BF16)],
        compiler_params=_params(("arbitrary", "arbitrary")),
        name="attn_latent",
    )(lam, qkv, qkv, qkv, cache_k, cache_v, cos, sin, subln_w.reshape(1, HEAD_DIM))


def _prefix_rows(x):
    row = lax.broadcasted_iota(jnp.int32, x.shape, 0)
    for j in range(N_LEVELS):
        s = 1 << j
        x = x + jnp.where(row >= s, pltpu.roll(x, s, 0), 0.0)
    return x


def _block_boundary(x, m):
    n, lanes = x.shape
    w = 2 * m
    if w >= 8:
        y = x.reshape(n // w, w, lanes)
        return jnp.broadcast_to(y[:, m - 1:m, :], y.shape).reshape(n, lanes)
    y = x.reshape(n // 8, 8, lanes)
    sub = lax.broadcasted_iota(jnp.int32, y.shape, 1)
    out = None
    for grp in range(8 // w):
        src = grp * w + m - 1
        b = jnp.broadcast_to(y[:, src:src + 1, :], y.shape)
        out = b if out is None else jnp.where(sub >= grp * w, b, out)
    return out.reshape(n, lanes)


def _scan_constants():
    t = np.arange(CHUNK)
    block = [(t[:, None] >> j) == (t[None, :] >> j) for j in range(N_LEVELS + 1)]
    fwd = [block[0]]
    for j in range(N_LEVELS):
        m = 1 << j
        up = (t & m) != 0
        fwd.append(block[j + 1] & up[:, None] & ~up[None, :])
    pair = np.stack([np.stack(fwd), np.stack([a.T for a in fwd])])
    return dict(block=jnp.asarray(np.stack(block).astype(np.float32)),
                pair=jnp.asarray(pair.astype(np.float32)))


def _gate_norm_out(o, gate, w):
    y = o * lax.rsqrt(jnp.mean(o * o, axis=-1, keepdims=True) + EPS) * w
    return (y * _silu(gate)).astype(BF16)


def _hgrn_chunk(q, z, v, lb, pair_ref, st, rev):
    e = jnp.exp(-jnp.abs(z))
    r = 1.0 / (1.0 + e)
    pos = z >= 0
    sig = jnp.where(pos, r, e * r)
    nsig = jnp.where(pos, e * r, r)
    f = lb + (1.0 - lb) * sig
    key = (1.0 - lb) * nsig
    lf = jnp.log2(f)
    incl = _prefix_rows(lf)
    base = incl - lf if rev else incl
    vb = v.astype(BF16)

    a = pair_ref[0] * _dot_nt(q.astype(BF16), key.astype(BF16))
    for lvl in range(N_LEVELS):
        dm = base - _block_boundary(incl, 1 << lvl)
        em = jnp.exp2(-jnp.abs(dm))
        a = a + pair_ref[lvl + 1] * _dot_nt((q * em).astype(BF16), (key * em).astype(BF16))

    tot = incl[CHUNK - 1:CHUNK, :]
    if rev:
        e_in = jnp.exp2(tot - base)
        e_out = jnp.exp2(base)
    else:
        e_in = jnp.exp2(incl)
        e_out = jnp.exp2(tot - incl)
    o = _dot(a.astype(BF16), vb) + _dot_nt((q * e_in).astype(BF16), st.astype(BF16))
    st_new = jnp.exp2(tot) * st + _dot_tn(vb, (key * e_out).astype(BF16))
    return o, st_new


def _hgrn_kernel(*refs, has_init, emit_state):
    refs = list(refs)
    q_ref, zf_ref, zb_ref, i_ref, g_ref, lb_ref, w_ref, pair_ref = refs[:8]
    refs = refs[8:]
    s0_ref = refs.pop(0) if has_init else None
    o_ref = refs.pop(0)
    s_out_ref = refs.pop(0) if emit_state else None
    of_ref, ob_ref, st_ref = refs

    n = q_ref.shape[0]
    nc = n // CHUNK
    hp = q_ref.shape[1] // HEAD_DIM
    lanes = [slice(hh * HEAD_DIM, (hh + 1) * HEAD_DIM) for hh in range(hp)]
    for d in range(2):
        for hh in range(hp):
            st_ref[d, hh] = (s0_ref[d, hh].T if has_init
                             else jnp.zeros((HEAD_DIM, HEAD_DIM), F32))

    def body(c, carry):
        rf = pl.ds(pl.multiple_of(c * CHUNK, CHUNK), CHUNK)
        rb = pl.ds(pl.multiple_of((nc - 1 - c) * CHUNK, CHUNK), CHUNK)
        scale = HEAD_DIM ** -0.5
        for hh, sl in enumerate(lanes):
            load = lambda ref, rows: ref[rows, sl].astype(F32)
            lb = lb_ref[hh]
            o, s = _hgrn_chunk(load(q_ref, rf) * scale, load(zf_ref, rf), load(i_ref, rf), lb,
                               pair_ref.at[0], st_ref[0, hh], False)
            of_ref[rf, sl] = o
            st_ref[0, hh] = s
            o, s = _hgrn_chunk(load(q_ref, rb) * scale, load(zb_ref, rb), load(i_ref, rb), lb,
                               pair_ref.at[1], st_ref[1, hh], True)
            ob_ref[rb, sl] = o
            st_ref[1, hh] = s
        return carry

    lax.fori_loop(0, nc, body, 0, unroll=min(nc, HGRN_UNROLL))
    for hh, sl in enumerate(lanes):
        o_ref[:, sl] = _gate_norm_out(of_ref[:, sl] + ob_ref[:, sl], g_ref[:, sl].astype(F32),
                                      w_ref[...])
        if emit_state:
            for d in range(2):
                s_out_ref[d, hh] = st_ref[d, hh].T


def hgrn_scan(proj, lb, norm_w, consts, seq_len, state_in, layer_idx, emit_state):
    r = proj.shape[0]
    b = r // seq_len
    has_init = state_in is not None
    hp = max(1, min(N_HEADS, HGRN_UNROLL // (seq_len // CHUNK)))
    nhb = N_HEADS // hp
    blk = (seq_len, hp * HEAD_DIM)
    col = lambda k: pl.BlockSpec(blk, lambda i, h, k=k: (i, k * nhb + h))
    state_blk = (None, None, 2, hp, HEAD_DIM, HEAD_DIM)
    state_spec = pl.BlockSpec(state_blk, lambda i, h: (i, layer_idx, 0, h, 0, 0))
    in_specs = [col(0), col(1), col(2), col(3), col(4),
                pl.BlockSpec((hp, 1, HEAD_DIM), lambda i, h: (h, 0, 0)),
                pl.BlockSpec((1, HEAD_DIM), lambda i, h: (0, 0)),
                pl.BlockSpec((2, N_LEVELS + 1, CHUNK, CHUNK), lambda i, h: (0, 0, 0, 0))]
    args = [proj, proj, proj, proj, proj, lb.reshape(N_HEADS, 1, HEAD_DIM),
            norm_w.reshape(1, HEAD_DIM), consts['pair']]
    if has_init:
        in_specs.append(state_spec)
        args.append(state_in)
    out_specs = [pl.BlockSpec(blk, lambda i, h: (i, h))]
    out_shape = [jax.ShapeDtypeStruct((r, D_MODEL), BF16)]
    if emit_state:
        out_specs.append(pl.BlockSpec(state_blk, lambda i, h: (i, 0, 0, h, 0, 0)))
        out_shape.append(jax.ShapeDtypeStruct((b, 1, 2, N_HEADS, HEAD_DIM, HEAD_DIM), F32))
    kern = functools.partial(_hgrn_kernel, has_init=has_init, emit_state=emit_state)
    res = pl.pallas_call(
        kern,
        grid=(b, nhb),
        in_specs=in_specs,
        out_specs=out_specs,
        out_shape=out_shape,
        scratch_shapes=[pltpu.VMEM(blk, F32), pltpu.VMEM(blk, F32),
                        pltpu.VMEM((2, hp, HEAD_DIM, HEAD_DIM), F32)],
        compiler_params=_params(("arbitrary", "arbitrary")),
        name="hgrn_scan",
    )(*args)
    return res if emit_state else (res[0], None)


def _lane_column(x, lane_idx):
    lane = lax.broadcasted_iota(jnp.int32, x.shape, 1)
    colv = jnp.sum(jnp.where(lane == lane_idx, x, 0.0), axis=1, keepdims=True)
    return jnp.broadcast_to(colv, x.shape)


def _gdn_prepare(items, block_ref):
    row = lax.broadcasted_iota(jnp.int32, (CHUNK, CHUNK), 0)
    colm = lax.broadcasted_iota(jnp.int32, (CHUNK, CHUNK), 1)
    pre = []
    for q, k, v, g, beta, kk, qk, rev in items:
        g_end = g[0:1, :] if rev else g[CHUNK - 1:CHUNK, :]
        within = (row <= colm) if rev else (row >= colm)
        strict = (row < colm) if rev else (row > colm)
        decay = jnp.exp2(jnp.where(within, g - g.T, -jnp.inf))
        a = jnp.where(strict, beta * kk * decay, 0.0)
        eg = jnp.exp2(g)
        x = jnp.concatenate([v * beta, k * (beta * eg)], axis=1).astype(BF16)
        p = (qk * decay).astype(BF16)
        ke = (k * jnp.exp2(g_end - g)).astype(BF16)
        pre.append((a, x, p, ke, q * eg, jnp.exp2(g_end)))
    ts = [block_ref[0] - it[0] * (block_ref[1] - block_ref[0]) for it in pre]
    for lvl in range(1, N_LEVELS):
        sel = block_ref[lvl + 1] - block_ref[lvl]
        tbs = [t.astype(BF16) for t in ts]
        lts = [_dot((it[0] * sel).astype(BF16), tb).astype(BF16) for it, tb in zip(pre, tbs)]
        ts = [t - _dot(tb, lt) for t, tb, lt in zip(ts, tbs, lts)]
    wus = [_dot(t.astype(BF16), it[1]).astype(BF16) for t, it in zip(ts, pre)]
    pwus = [_dot(it[2], wu) for it, wu in zip(pre, wus)]
    kwus = [_dot_tn(it[3], wu) for it, wu in zip(pre, wus)]
    out = []
    for it, pwu, kwu in zip(pre, pwus, kwus):
        qs = (it[4] - pwu[:, HEAD_DIM:]).astype(BF16)
        out.append((qs, pwu[:, :HEAD_DIM], kwu[:, HEAD_DIM:].astype(BF16), kwu[:, :HEAD_DIM], it[5]))
    return out


def _gdn_kernel(*refs, has_init, emit_state, hp):
    refs = list(refs)
    (q_ref, k_ref, v_ref, gate_ref, ab_ref, cq_ref, ck_ref, cv_ref, rate_ref, bias_ref,
     w_ref, block_ref) = refs[:12]
    refs = refs[12:]
    s0_ref = refs.pop(0) if has_init else None
    o_ref = refs.pop(0)
    s_out_ref = refs.pop(0) if emit_state else None
    qn_ref, kn_ref, vn_ref, qs_ref, ms_ref, o_acc_ref, b_ref, dec_ref, st_ref = refs

    n = q_ref.shape[0]
    nc = n // CHUNK
    group = max(1, min(nc, GDN_CHAINS // (2 * hp)))
    head0 = pl.program_id(1) * hp
    pos = lax.broadcasted_iota(jnp.int32, (n, hp * HEAD_DIM), 0)

    def conv_silu(x_ref, cw_ref):
        x = x_ref[...].astype(F32)
        prev = jnp.where(pos == 0, 0.0, pltpu.roll(x, 1, 0))
        nxt = jnp.where(pos == n - 1, 0.0, pltpu.roll(x, n - 1, 0))
        return _silu(prev * cw_ref[0:1, :] + x * cw_ref[1:2, :] + nxt * cw_ref[2:3, :])

    def l2norm(x):
        return x * lax.rsqrt(jnp.sum(x * x, axis=-1, keepdims=True) + 1e-6)

    qc = conv_silu(q_ref, cq_ref)
    kc = conv_silu(k_ref, ck_ref)
    vn_ref[...] = conv_silu(v_ref, cv_ref)
    for hh in range(hp):
        sl = slice(hh * HEAD_DIM, (hh + 1) * HEAD_DIM)
        qn_ref[:, sl] = l2norm(qc[:, sl]) * (HEAD_DIM ** -0.5)
        kn_ref[:, sl] = l2norm(kc[:, sl])
        for d in range(2):
            st_ref[d, hh] = s0_ref[d, hh] if has_init else jnp.zeros((HEAD_DIM, HEAD_DIM), F32)

    def chunk_rows(c):
        return pl.ds(pl.multiple_of(c * CHUNK, CHUNK), CHUNK)

    def prepare_body(i, carry):
        items, dest = [], []
        lane = lax.broadcasted_iota(jnp.int32, (CHUNK, HEAD_DIM), 1)
        for grp in range(group):
            rows = chunk_rows(i * group + grp)
            ab = ab_ref[rows, :]
            la = -(rate_ref[...] * LOG2E) * jax.nn.softplus(ab + bias_ref[...])
            sg_all = jax.nn.sigmoid(ab)
            incl = _prefix_rows(la)
            tot = incl[CHUNK - 1:CHUNK, :]
            g_all = jnp.where(lane >= N_HEADS, tot - incl + la, incl)
            for hh in range(hp):
                sl = slice(hh * HEAD_DIM, (hh + 1) * HEAD_DIM)
                q = qn_ref[rows, sl]
                k = kn_ref[rows, sl]
                v = vn_ref[rows, sl]
                kb = k.astype(BF16)
                kk = _dot_nt(kb, kb)
                qk = _dot_nt(q.astype(BF16), kb)
                for d in range(2):
                    g = _lane_column(g_all, d * N_HEADS + head0 + hh)
                    beta = _lane_column(sg_all, (2 + d) * N_HEADS + head0 + hh)
                    items.append((q, k, v, g, beta, kk, qk, d == 1))
                    dest.append((d, hh, rows))
        for (d, hh, rows), (qs, o0, ms, b, dec) in zip(dest, _gdn_prepare(items, block_ref)):
            qs_ref[d, hh, rows, :] = qs
            ms_ref[d, hh, rows, :] = ms
            o_acc_ref[d, hh, rows, :] = o0
            b_ref[d, hh, rows, :] = b
            dec_ref[d, hh, rows, :] = jnp.broadcast_to(dec, (CHUNK, HEAD_DIM))
        return carry

    lax.fori_loop(0, nc // group, prepare_body, 0)

    def scan_body(c, carry):
        for hh in range(hp):
            for d, cc in ((0, c), (1, nc - 1 - c)):
                rows = chunk_rows(cc)
                s = st_ref[d, hh]
                sb = s.astype(BF16)
                o_acc_ref[d, hh, rows, :] += _dot(qs_ref[d, hh, rows, :], sb)
                st_ref[d, hh] = (dec_ref[d, hh, rows, :] * s - _dot(ms_ref[d, hh, rows, :], sb)
                                 + b_ref[d, hh, rows, :])
        return carry

    lax.fori_loop(0, nc, scan_body, 0)
    for hh in range(hp):
        sl = slice(hh * HEAD_DIM, (hh + 1) * HEAD_DIM)
        o_ref[:, sl] = _gate_norm_out(o_acc_ref[0, hh] + o_acc_ref[1, hh],
                                      gate_ref[:, sl].astype(F32), w_ref[...])
    if emit_state:
        for d in range(2):
            for hh in range(hp):
                s_out_ref[d, hh] = st_ref[d, hh]


def gdn_scan(proj, ab, conv_w, rate, bias, norm_w, consts, seq_len, state_in, layer_idx,
             emit_state):
    r = proj.shape[0]
    b = r // seq_len
    has_init = state_in is not None
    nc = seq_len // CHUNK
    hp = GDN_HEADS_PER_STEP
    blk = (seq_len, hp * HEAD_DIM)
    nhb = N_HEADS // hp
    col = lambda k: pl.BlockSpec(blk, lambda i, h, k=k: (i, k * nhb + h))
    ccol = lambda k: pl.BlockSpec((3, hp * HEAD_DIM), lambda i, h, k=k: (0, k * nhb + h))
    vec = pl.BlockSpec((1, HEAD_DIM), lambda i, h: (0, 0))
    state_blk = (None, None, 2, hp, HEAD_DIM, HEAD_DIM)
    in_specs = [col(0), col(1), col(2), col(3),
                pl.BlockSpec((seq_len, HEAD_DIM), lambda i, h: (i, 0)),
                ccol(0), ccol(1), ccol(2), vec, vec, vec,
                pl.BlockSpec((N_LEVELS + 1, CHUNK, CHUNK), lambda i, h: (0, 0, 0))]
    args = [proj, proj, proj, proj, ab, conv_w, conv_w, conv_w, rate, bias,
            norm_w.reshape(1, HEAD_DIM), consts['block']]
    if has_init:
        in_specs.append(pl.BlockSpec(state_blk, lambda i, h: (i, layer_idx, 0, h, 0, 0)))
        args.append(state_in)
    out_specs = [pl.BlockSpec(blk, lambda i, h: (i, h))]
    out_shape = [jax.ShapeDtypeStruct((r, D_MODEL), BF16)]
    if emit_state:
        out_specs.append(pl.BlockSpec(state_blk, lambda i, h: (i, 0, 0, h, 0, 0)))
        out_shape.append(jax.ShapeDtypeStruct((b, 1, 2, N_HEADS, HEAD_DIM, HEAD_DIM), F32))
    kern = functools.partial(_gdn_kernel, has_init=has_init, emit_state=emit_state, hp=hp)
    per_head = (2, hp, seq_len, HEAD_DIM)
    res = pl.pallas_call(
        kern,
        grid=(b, nhb),
        in_specs=in_specs,
        out_specs=out_specs,
        out_shape=out_shape,
        scratch_shapes=[pltpu.VMEM(blk, F32)] * 3
        + [pltpu.VMEM(per_head, BF16)] * 2 + [pltpu.VMEM(per_head, F32)] * 3
        + [pltpu.VMEM((2, hp, HEAD_DIM, HEAD_DIM), F32)],
        compiler_params=_params(("arbitrary", "arbitrary")),
        name="gdn_scan",
    )(*args)
    return res if emit_state else (res[0], None)


def _rope_tables(n):
    rows = n // GRID_W
    row = jnp.repeat(jnp.arange(rows, dtype=F32), GRID_W)
    colp = jnp.tile(jnp.arange(GRID_W, dtype=F32), rows)
    nf = DA_HD // 4
    inv = ROPE_BASE ** (-jnp.arange(nf, dtype=F32) / nf)
    ar = row[:, None] * inv
    ac = colp[:, None] * inv
    cr, sr, cc, sc = jnp.cos(ar), jnp.sin(ar), jnp.cos(ac), jnp.sin(ac)
    cos = jnp.concatenate([cr, cr, cc, cc] * 2, axis=1)
    sin = jnp.concatenate([-sr, sr, -sc, sc] * 2, axis=1)
    return cos, sin


def _pad_halves(a, width, padded):
    zeros = jnp.zeros(a.shape[:-1] + (padded - width,), a.dtype)
    return jnp.concatenate([a[..., :width], zeros, a[..., width:], zeros], axis=-1)


def _trunk(x, mods, rows_per_mod, seq_len, P, cache):
    ctx_mode = cache is None
    depth = P['w_mod'].shape[0]
    n_attn = P['attn_w_in'].shape[0]
    kv, hgrn_state, gdn_state = None, None, None
    for i in range(depth):
        sh1, sc1, g1, sh2, sc2, g2 = mods[i]
        kind, j = i % 3, i // 3
        proj = functools.partial(norm_proj, x, P['norm_g'][i, 0], sc1, sh1,
                                 rows_per_mod=rows_per_mod)
        if kind == 0:
            qkv = proj(P['attn_w_in'][j], tn=1024, out_dtype=BF16)
            lam_init = 0.8 - 0.6 * math.exp(-0.3 * i)
            lp = P['attn_lambda'][j]
            lam = (jnp.exp(jnp.sum(lp[0] * lp[1])) - jnp.exp(jnp.sum(lp[2] * lp[3]))
                   + lam_init).reshape(1)
            if ctx_mode:
                o, kv = attn_context(qkv, lam, P['attn_subln'][j], seq_len, 1.0 - lam_init, j,
                                     n_attn, kv)
            else:
                o = attn_latent(qkv, cache['attn_k'], cache['attn_v'], j, cache['cos'],
                                cache['sin'], lam, P['attn_subln'][j], seq_len, 1.0 - lam_init)
            w_out = P['attn_w_out'][j]
        elif kind == 1:
            o, st = hgrn_scan(proj(P['hgrn_w_in'][j], tn=1024, out_dtype=BF16), P['lb_all'][i],
                              P['hgrn_norm'][j], P['consts'], seq_len,
                              None if ctx_mode else cache['hgrn'], j, ctx_mode)
            if ctx_mode:
                hgrn_state = st
            w_out = P['hgrn_w_out'][j]
        else:
            o, st = gdn_scan(proj(P['gdn_w_main'][j], tn=1024, out_dtype=BF16),
                             proj(P['gdn_w_tail'][j], tn=HEAD_DIM, out_dtype=F32),
                             P['gdn_conv'][j], P['gdn_rate'][j], P['gdn_bias'][j],
                             P['gdn_norm'][j], P['consts'], seq_len,
                             None if ctx_mode else cache['gdn'], j, ctx_mode)
            if ctx_mode:
                gdn_state = st
            w_out = P['gdn_w_out'][j]
        x = out_proj_residual(o, w_out, x, g1, rows_per_mod)
        x = conv_ffn_residual(x, P['norm_g'][i, 1], sc2, sh2, g2, P['ffn_w_up'][i],
                              P['ffn_conv'][i], P['ffn_conv_b'][i], P['ffn_w_down'][i],
                              P['final_g'], rows_per_mod, seq_len, i == depth - 1)
    return x, (kv, hgrn_state, gdn_state)


def kernel(x_prompt, x_sample, cache_attn_k, cache_attn_v, state_hgrn, state_gdn, c, c_ctx, norm_g, w_mod, b_mod, final_g, attn_w_in, attn_lambda, attn_subln, attn_w_out, hgrn_w_in, hgrn_lb, hgrn_norm, hgrn_w_out, gdn_w_in, gdn_conv, gdn_a_log, gdn_dt_bias, gdn_norm, gdn_w_out, ffn_w_up, ffn_conv, ffn_conv_b, ffn_w_down):
    batch, seq, d = x_prompt.shape
    dec_batch, dec_seq, _ = x_sample.shape
    depth = w_mod.shape[0]
    n_gdn = gdn_w_in.shape[0]

    lb_all = jnp.cumsum(jax.nn.softmax(hgrn_lb, axis=0), axis=0)
    lb_all = lb_all - lb_all[0]

    d_ff = ffn_w_down.shape[1]
    step_cols = FF_TILE * FFN_TILES_PER_STEP
    d_ff_pad = -(-d_ff // step_cols) * step_cols

    pad = jnp.zeros((n_gdn, HEAD_DIM - 2 * N_HEADS), F32)
    gdn_rate = jnp.concatenate([jnp.exp(gdn_a_log).reshape(n_gdn, 2 * N_HEADS), pad], axis=1)
    gdn_bias = jnp.concatenate([gdn_dt_bias.reshape(n_gdn, 2 * N_HEADS), pad], axis=1)
    tail = gdn_w_in[:, :, 4 * d:]
    tail = jnp.concatenate([tail, jnp.zeros((n_gdn, d, HEAD_DIM - tail.shape[2]), F32)], axis=2)

    P = dict(norm_g=norm_g, w_mod=w_mod, final_g=final_g,
             attn_w_in=attn_w_in.astype(BF16), attn_lambda=attn_lambda, attn_subln=attn_subln,
             attn_w_out=attn_w_out.astype(BF16),
             hgrn_w_in=hgrn_w_in.astype(BF16), lb_all=lb_all, hgrn_norm=hgrn_norm,
             hgrn_w_out=hgrn_w_out.astype(BF16),
             gdn_w_main=gdn_w_in[:, :, :4 * d].astype(BF16), gdn_w_tail=tail.astype(BF16),
             gdn_conv=gdn_conv, gdn_rate=gdn_rate.reshape(n_gdn, 1, HEAD_DIM),
             gdn_bias=gdn_bias.reshape(n_gdn, 1, HEAD_DIM), gdn_norm=gdn_norm,
             gdn_w_out=gdn_w_out.astype(BF16),
             ffn_w_up=_pad_halves(ffn_w_up.astype(BF16), d_ff, d_ff_pad),
             ffn_conv=_pad_halves(ffn_conv, d_ff, d_ff_pad),
             ffn_conv_b=_pad_halves(ffn_conv_b, d_ff, d_ff_pad),
             ffn_w_down=jnp.pad(ffn_w_down.astype(BF16), ((0, 0), (0, d_ff_pad - d_ff), (0, 0))),
             consts=_scan_constants())

    cond = jnp.zeros((16, d), F32).at[:dec_batch].set(c).at[dec_batch].set(c_ctx)
    mod = modulation_all(cond, w_mod, b_mod).reshape(depth, 16, 6, 1, d)
    mods_lat = [[mod[i, :dec_batch, t] for t in range(6)] for i in range(depth)]
    mods_ctx = [[mod[i, dec_batch:dec_batch + 1, t] for t in range(6)] for i in range(depth)]

    y_prompt, ((new_k, new_v), new_hgrn, new_gdn) = _trunk(
        x_prompt.reshape(batch * seq, d), mods_ctx, batch * seq, seq, P, None)

    cos, sin = _rope_tables(dec_seq)
    flat_k, flat_v = flatten_cache(cache_attn_k, cache_attn_v)
    cache = dict(attn_k=flat_k, attn_v=flat_v, hgrn=state_hgrn, gdn=state_gdn, cos=cos, sin=sin)
    y_sample, _ = _trunk(x_sample.reshape(dec_batch * dec_seq, d), mods_lat, dec_seq, dec_seq, P,
                         cache)

    return (y_prompt.reshape(batch, seq, d), y_sample.reshape(dec_batch, dec_seq, d),
            new_k, new_v, new_hgrn, new_gdn)
```

```python
import functools
import math

import jax
import jax.numpy as jnp
import numpy as np
from jax import lax
from jax.experimental import pallas as pl
from jax.experimental.pallas import tpu as pltpu

F32 = jnp.float32
BF16 = jnp.bfloat16

D_MODEL = 1024
N_HEADS = 8
HEAD_DIM = 128
DA_HD = 64
GRID_W = 64
ROPE_BASE = 10000.0
D_FF = 2816
EPS = 1e-6
LOG2E = 1.4426950408889634
CHUNK = 128
N_LEVELS = 7
ROW_TILE = 1024
FF_TILE = 256
FFN_TILES_PER_STEP = 6
Q_TILE = 256
SCORE_LOOKAHEAD = 8
HGRN_UNROLL = 8
GDN_CHAINS = 16
GDN_HEADS_PER_STEP = 4
VMEM_LIMIT = 56 * 1024 * 1024


def _params(sem, vmem=VMEM_LIMIT):
    return pltpu.CompilerParams(dimension_semantics=sem, vmem_limit_bytes=vmem)


def _dot(a, b):
    return jnp.dot(a, b, preferred_element_type=F32)


def _dot_nt(a, b):
    return lax.dot_general(a, b, (((1,), (1,)), ((), ())), preferred_element_type=F32)


def _dot_tn(a, b):
    return lax.dot_general(a, b, (((0,), (0,)), ((), ())), preferred_element_type=F32)


def _silu(x):
    return x * jax.nn.sigmoid(x)


def _norm_mod(x, g, sc, sh):
    y = x * lax.rsqrt(jnp.mean(x * x, axis=-1, keepdims=True) + EPS)
    return (y * g) * (1.0 + sc) + sh


def _mod_kernel(c_ref, w_ref, b_ref, o_ref):
    s = _silu(c_ref[...]).astype(BF16)
    o_ref[...] = _dot(s, w_ref[...].astype(BF16)) + b_ref[...]


def modulation_all(cond, w_mod, b_mod):
    depth, d, n = w_mod.shape
    tn = 1024
    return pl.pallas_call(
        _mod_kernel,
        grid=(depth, n // tn),
        in_specs=[pl.BlockSpec((16, d), lambda l, j: (0, 0)),
                  pl.BlockSpec((None, d, tn), lambda l, j: (l, 0, j)),
                  pl.BlockSpec((None, 1, tn), lambda l, j: (l, 0, j))],
        out_specs=pl.BlockSpec((None, 16, tn), lambda l, j: (l, 0, j)),
        out_shape=jax.ShapeDtypeStruct((depth, 16, n), F32),
        compiler_params=_params(("arbitrary", "arbitrary")),
        name="modulation",
    )(cond, w_mod, b_mod.reshape(depth, 1, n))


def _proj_kernel(x_ref, g_ref, sc_ref, sh_ref, w_ref, o_ref, h_ref):
    @pl.when(pl.program_id(1) == 0)
    def _():
        h_ref[...] = _norm_mod(x_ref[...], g_ref[...], sc_ref[...], sh_ref[...]).astype(BF16)

    o_ref[...] = _dot(h_ref[...], w_ref[...]).astype(o_ref.dtype)


def norm_proj(x, g, sc, sh, w, rows_per_mod, tn, out_dtype):
    r, d = x.shape
    n = w.shape[1]
    tm = ROW_TILE
    mod_spec = pl.BlockSpec((None, 1, d), lambda i, j: ((i * tm) // rows_per_mod, 0, 0))
    return pl.pallas_call(
        _proj_kernel,
        grid=(r // tm, n // tn),
        in_specs=[pl.BlockSpec((tm, d), lambda i, j: (i, 0)),
                  pl.BlockSpec((1, d), lambda i, j: (0, 0)),
                  mod_spec, mod_spec,
                  pl.BlockSpec((d, tn), lambda i, j: (0, j))],
        out_specs=pl.BlockSpec((tm, tn), lambda i, j: (i, j)),
        out_shape=jax.ShapeDtypeStruct((r, n), out_dtype),
        scratch_shapes=[pltpu.VMEM((tm, d), BF16)],
        compiler_params=_params(("arbitrary", "arbitrary")),
        name="norm_proj",
    )(x, g.reshape(1, d), sc, sh, w)


def _out_proj_kernel(o_ref, w_ref, x_ref, gate_ref, y_ref):
    y_ref[...] = x_ref[...] + gate_ref[...] * _dot(o_ref[...], w_ref[...])


def out_proj_residual(o, w, x, gate, rows_per_mod):
    r, d = x.shape
    tm = ROW_TILE
    return pl.pallas_call(
        _out_proj_kernel,
        grid=(r // tm,),
        in_specs=[pl.BlockSpec((tm, d), lambda i: (i, 0)),
                  pl.BlockSpec((d, d), lambda i: (0, 0)),
                  pl.BlockSpec((tm, d), lambda i: (i, 0)),
                  pl.BlockSpec((None, 1, d), lambda i: ((i * tm) // rows_per_mod, 0, 0))],
        out_specs=pl.BlockSpec((tm, d), lambda i: (i, 0)),
        out_shape=jax.ShapeDtypeStruct((r, d), F32),
        compiler_params=_params(("arbitrary",)),
        name="out_proj",
    )(o, w, x, gate)


def _ffn_kernel(*refs, seq_len, final_norm):
    x_ref, g_ref, sc_ref, sh_ref, gate_ref = refs[:5]
    nt = FFN_TILES_PER_STEP
    wv_refs, wg_refs, cv_refs, cg_refs, bv_refs, bg_refs, wd_refs = (
        refs[5 + k * nt:5 + (k + 1) * nt] for k in range(7))
    fg_ref, y_ref, h_ref, acc_ref = refs[5 + 7 * nt:]
    j = pl.program_id(1)

    @pl.when(j == 0)
    def _():
        h_ref[...] = _norm_mod(x_ref[...], g_ref[...], sc_ref[...], sh_ref[...]).astype(BF16)
        acc_ref[...] = jnp.zeros_like(acc_ref)

    tm = h_ref.shape[0]
    h = h_ref[...]
    pos = lax.broadcasted_iota(jnp.int32, (tm, FF_TILE), 0) & (seq_len - 1)
    first = pos == 0
    last = pos == seq_len - 1

    def conv(u, cw_ref, b_ref):
        prev = jnp.where(first, 0.0, pltpu.roll(u, 1, 0))
        nxt = jnp.where(last, 0.0, pltpu.roll(u, tm - 1, 0))
        return prev * cw_ref[0:1, :] + u * cw_ref[1:2, :] + nxt * cw_ref[2:3, :] + b_ref[...]

    ups = [(_dot(h, wv_refs[t][...]), _dot(h, wg_refs[t][...])) for t in range(nt)]
    for t in range(nt):
        val = conv(ups[t][0], cv_refs[t], bv_refs[t])
        gte = conv(ups[t][1], cg_refs[t], bg_refs[t])
        act = (val * _silu(gte)).astype(BF16)
        acc_ref[...] += _dot(act, wd_refs[t][...])

    @pl.when(j == pl.num_programs(1) - 1)
    def _():
        y = x_ref[...] + gate_ref[...] * acc_ref[...]
        if final_norm:
            y = y * lax.rsqrt(jnp.mean(y * y, axis=-1, keepdims=True) + EPS) * fg_ref[...]
        y_ref[...] = y


def conv_ffn_residual(x, g, sc, sh, gate, w_up, conv_w, conv_b, w_down, final_g, rows_per_mod,
                      seq_len, final_norm):
    r, d = x.shape
    tm = ROW_TILE
    nt = FFN_TILES_PER_STEP
    nf = w_down.shape[0] // FF_TILE
    mod_spec = pl.BlockSpec((None, 1, d), lambda i, j: ((i * tm) // rows_per_mod, 0, 0))
    kern = functools.partial(_ffn_kernel, seq_len=seq_len, final_norm=final_norm)
    conv_b = conv_b.reshape(1, -1)

    def cols(rows, half):
        return [pl.BlockSpec((rows, FF_TILE), lambda i, j, t=t: (0, half * nf + nt * j + t))
                for t in range(nt)]

    in_specs = ([pl.BlockSpec((tm, d), lambda i, j: (i, 0)),
                 pl.BlockSpec((1, d), lambda i, j: (0, 0)),
                 mod_spec, mod_spec, mod_spec]
                + cols(d, 0) + cols(d, 1) + cols(3, 0) + cols(3, 1) + cols(1, 0) + cols(1, 1)
                + [pl.BlockSpec((FF_TILE, d), lambda i, j, t=t: (nt * j + t, 0)) for t in range(nt)]
                + [pl.BlockSpec((1, d), lambda i, j: (0, 0))])
    args = ([x, g.reshape(1, d), sc, sh, gate] + [w_up] * (2 * nt) + [conv_w] * (2 * nt)
            + [conv_b] * (2 * nt) + [w_down] * nt + [final_g.reshape(1, d)])
    return pl.pallas_call(
        kern,
        grid=(r // tm, nf // nt),
        in_specs=in_specs,
        out_specs=pl.BlockSpec((tm, d), lambda i, j: (i, 0)),
        out_shape=jax.ShapeDtypeStruct((r, d), F32),
        scratch_shapes=[pltpu.VMEM((tm, d), BF16), pltpu.VMEM((tm, d), F32)],
        compiler_params=_params(("arbitrary", "arbitrary")),
        name="conv_ffn",
    )(*args)


def _ones_column_block(rows):
    lane = lax.broadcasted_iota(jnp.int32, (rows, HEAD_DIM), 1)
    return jnp.where(lane == 0, 1.0, 0.0).astype(BF16)


def _diff_attend(tiles, lam, emit):
    chains = [(t, c) for t in range(len(tiles)) for c in range(2)]
    q_cache = {}

    def scores(t, c):
        if t not in q_cache:
            q_cache[t] = tiles[t][0]()
        q = q_cache[t]
        lane = lax.broadcasted_iota(jnp.int32, q.shape, 1)
        qc = jnp.where((lane < DA_HD) == (c == 0), q, 0.0).astype(BF16)
        return _dot_nt(qc, tiles[t][1])

    ahead = SCORE_LOOKAHEAD
    pending = [scores(*ch) for ch in chains[:ahead]]
    first = None
    for idx, (t, c) in enumerate(chains):
        if idx + ahead < len(chains):
            pending.append(scores(*chains[idx + ahead]))
        s = pending.pop(0)
        e = jnp.exp2(s - jnp.max(s, axis=-1, keepdims=True)).astype(BF16)
        pv = _dot(e, tiles[t][2])
        o = pv[:, :HEAD_DIM] / pv[:, HEAD_DIM:HEAD_DIM + 1]
        if c == 0:
            first = o
        else:
            emit(t, first - lam * o)


def _subln(o, w, post_scale):
    return o * lax.rsqrt(jnp.mean(o * o, axis=-1, keepdims=True) + 1e-5) * w * post_scale


def _attn_ctx_kernel(*refs, post_scale, layer_idx, first_layer):
    if first_layer:
        lam_ref, q_ref, k_ref, v_ref, w_ref, o_ref, nk_ref, nv_ref = refs
    else:
        lam_ref, q_ref, k_ref, v_ref, w_ref, _, _, o_ref, nk_ref, nv_ref = refs
    lam = lam_ref[0]
    n = q_ref.shape[0]
    ones = _ones_column_block(n)
    scale = DA_HD ** -0.5 * LOG2E
    for h in range(N_HEADS):
        sl = slice(h * HEAD_DIM, (h + 1) * HEAD_DIM)
        v1 = jnp.concatenate([v_ref[:, sl], ones], axis=1)

        def emit(t, o, sl=sl):
            o_ref[:, sl] = _subln(o, w_ref[...], post_scale).astype(BF16)

        _diff_attend([(lambda sl=sl: q_ref[:, sl].astype(F32) * scale, k_ref[:, sl], v1)], lam,
                     emit)

    kv_shape = (n, N_HEADS, HEAD_DIM)
    if first_layer:
        for l in range(nk_ref.shape[0]):
            if l == layer_idx:
                nk_ref[l] = k_ref[...].astype(F32).reshape(kv_shape)
                nv_ref[l] = v_ref[...].astype(F32).reshape(kv_shape)
            else:
                nk_ref[l] = jnp.zeros(kv_shape, F32)
                nv_ref[l] = jnp.zeros(kv_shape, F32)
    else:
        nk_ref[...] = k_ref[...].astype(F32).reshape(kv_shape)
        nv_ref[...] = v_ref[...].astype(F32).reshape(kv_shape)


def attn_context(qkv, lam, subln_w, seq_len, post_scale, layer_idx, n_layers, kv_prev):
    r = qkv.shape[0]
    b = r // seq_len
    d = D_MODEL
    first_layer = kv_prev is None
    kern = functools.partial(_attn_ctx_kernel, post_scale=post_scale, layer_idx=layer_idx,
                             first_layer=first_layer)
    in_specs = [pl.BlockSpec(memory_space=pltpu.SMEM),
                pl.BlockSpec((seq_len, d), lambda i: (i, 0)),
                pl.BlockSpec((seq_len, d), lambda i: (i, 1)),
                pl.BlockSpec((seq_len, d), lambda i: (i, 2)),
                pl.BlockSpec((1, HEAD_DIM), lambda i: (0, 0))]
    args = [lam, qkv, qkv, qkv, subln_w.reshape(1, HEAD_DIM)]
    kv_sds = jax.ShapeDtypeStruct((b, n_layers, seq_len, N_HEADS, HEAD_DIM), F32)
    if first_layer:
        kv_spec = pl.BlockSpec((None, n_layers, seq_len, N_HEADS, HEAD_DIM),
                               lambda i: (i, 0, 0, 0, 0))
        aliases = {}
    else:
        kv_spec = pl.BlockSpec((None, None, seq_len, N_HEADS, HEAD_DIM),
                               lambda i: (i, layer_idx, 0, 0, 0))
        in_specs += [pl.BlockSpec(memory_space=pl.ANY)] * 2
        args += list(kv_prev)
        aliases = {5: 1, 6: 2}
    o, nk, nv = pl.pallas_call(
        kern,
        grid=(b,),
        in_specs=in_specs,
        out_specs=[pl.BlockSpec((seq_len, d), lambda i: (i, 0)), kv_spec, kv_spec],
        out_shape=[jax.ShapeDtypeStruct((r, d), BF16), kv_sds, kv_sds],
        input_output_aliases=aliases,
        compiler_params=_params(("arbitrary",)),
        name="attn_context",
    )(*args)
    return o, (nk, nv)


def _rope(x, cos, sin):
    lane = lax.broadcasted_iota(jnp.int32, x.shape, 1)
    partner = jnp.where((lane & 16) == 0, pltpu.roll(x, HEAD_DIM - 16, 1), pltpu.roll(x, 16, 1))
    return x * cos + partner * sin


def _attn_lat_kernel(lam_ref, q_ref, k_ref, v_ref, ck_ref, cv_ref, cos_ref, sin_ref, w_ref, o_ref,
                     kcat_ref, vcat_ref, *, post_scale):
    lam = lam_ref[0]
    past = ck_ref.shape[0]
    n = q_ref.shape[0]
    kcat_ref[0:past, :] = ck_ref[...]
    kcat_ref[past:past + n, :] = _rope(k_ref[...].astype(F32), cos_ref[...],
                                       sin_ref[...]).astype(BF16)
    vcat_ref[0:past, 0:HEAD_DIM] = cv_ref[...]
    vcat_ref[past:past + n, 0:HEAD_DIM] = v_ref[...]
    vcat_ref[:, HEAD_DIM:] = _ones_column_block(past + n)
    scale = DA_HD ** -0.5 * LOG2E

    def q_tile(t):
        rows = slice(t * Q_TILE, (t + 1) * Q_TILE)
        return _rope(q_ref[rows, :].astype(F32), cos_ref[rows, :], sin_ref[rows, :]) * scale

    def emit(t, o):
        o_ref[t * Q_TILE:(t + 1) * Q_TILE, :] = _subln(o, w_ref[...], post_scale).astype(BF16)

    k = kcat_ref[...]
    v1 = vcat_ref[...]
    _diff_attend([(functools.partial(q_tile, t), k, v1) for t in range(n // Q_TILE)], lam, emit)


def _flatten_heads_kernel(k_ref, v_ref, ok_ref, ov_ref):
    shape = ok_ref.shape
    ok_ref[...] = k_ref[...].reshape(shape).astype(BF16)
    ov_ref[...] = v_ref[...].reshape(shape).astype(BF16)


def flatten_cache(cache_k, cache_v):
    b, l, past, h, e = cache_k.shape
    in_spec = pl.BlockSpec((None, None, past, h, e), lambda i, j: (i, j, 0, 0, 0))
    out_spec = pl.BlockSpec((None, None, past, h * e), lambda i, j: (i, j, 0, 0))
    sds = jax.ShapeDtypeStruct((b, l, past, h * e), BF16)
    return pl.pallas_call(
        _flatten_heads_kernel,
        grid=(b, l),
        in_specs=[in_spec, in_spec],
        out_specs=[out_spec, out_spec],
        out_shape=[sds, sds],
        compiler_params=_params(("arbitrary", "arbitrary")),
        name="flatten_cache",
    )(cache_k, cache_v)


def attn_latent(qkv, cache_k, cache_v, layer_idx, cos, sin, lam, subln_w, seq_len, post_scale):
    r = qkv.shape[0]
    b = r // seq_len
    past = cache_k.shape[2]
    kern = functools.partial(_attn_lat_kernel, post_scale=post_scale)
    blk = (seq_len, HEAD_DIM)
    cache_spec = pl.BlockSpec((None, None, past, HEAD_DIM), lambda i, h: (i, layer_idx, 0, h))
    tab_spec = pl.BlockSpec((seq_len, HEAD_DIM), lambda i, h: (0, 0))
    return pl.pallas_call(
        kern,
        grid=(b, N_HEADS),
        in_specs=[pl.BlockSpec(memory_space=pltpu.SMEM),
                  pl.BlockSpec(blk, lambda i, h: (i, h)),
                  pl.BlockSpec(blk, lambda i, h: (i, N_HEADS + h)),
                  pl.BlockSpec(blk, lambda i, h: (i, 2 * N_HEADS + h)),
                  cache_spec, cache_spec, tab_spec, tab_spec,
                  pl.BlockSpec((1, HEAD_DIM), lambda i, h: (0, 0))],
        out_specs=pl.BlockSpec(blk, lambda i, h: (i, h)),
        out_shape=jax.ShapeDtypeStruct((r, D_MODEL), BF16),
        scratch_shapes=[pltpu.VMEM((past + seq_len, HEAD_DIM), BF16),
                        pltpu.VMEM((past + seq_len, 2 * HEAD_DIM), BF16)],
        compiler_params=_params(("arbitrary", "arbitrary")),
        name="attn_latent",
    )(lam, qkv, qkv, qkv, cache_k, cache_v, cos, sin, subln_w.reshape(1, HEAD_DIM))


def _prefix_rows(x):
    row = lax.broadcasted_iota(jnp.int32, x.shape, 0)
    for j in range(N_LEVELS):
        s = 1 << j
        x = x + jnp.where(row >= s, pltpu.roll(x, s, 0), 0.0)
    return x


def _block_boundary(x, m):
    n, lanes = x.shape
    w = 2 * m
    if w >= 8:
        y = x.reshape(n // w, w, lanes)
        return jnp.broadcast_to(y[:, m - 1:m, :], y.shape).reshape(n, lanes)
    y = x.reshape(n // 8, 8, lanes)
    sub = lax.broadcasted_iota(jnp.int32, y.shape, 1)
    out = None
    for grp in range(8 // w):
        src = grp * w + m - 1
        b = jnp.broadcast_to(y[:, src:src + 1, :], y.shape)
        out = b if out is None else jnp.where(sub >= grp * w, b, out)
    return out.reshape(n, lanes)


def _scan_constants():
    t = np.arange(CHUNK)
    block = [(t[:, None] >> j) == (t[None, :] >> j) for j in range(N_LEVELS + 1)]
    fwd = [block[0]]
    for j in range(N_LEVELS):
        m = 1 << j
        up = (t & m) != 0
        fwd.append(block[j + 1] & up[:, None] & ~up[None, :])
    pair = np.stack([np.stack(fwd), np.stack([a.T for a in fwd])])
    return dict(block=jnp.asarray(np.stack(block).astype(np.float32)),
                pair=jnp.asarray(pair.astype(np.float32)))


def _gate_norm_out(o, gate, w):
    y = o * lax.rsqrt(jnp.mean(o * o, axis=-1, keepdims=True) + EPS) * w
    return (y * _silu(gate)).astype(BF16)


def _hgrn_chunk(q, z, v, lb, pair_ref, st, rev):
    e = jnp.exp(-jnp.abs(z))
    r = 1.0 / (1.0 + e)
    pos = z >= 0
    sig = jnp.where(pos, r, e * r)
    nsig = jnp.where(pos, e * r, r)
    f = lb + (1.0 - lb) * sig
    key = (1.0 - lb) * nsig
    lf = jnp.log2(f)
    incl = _prefix_rows(lf)
    base = incl - lf if rev else incl
    vb = v.astype(BF16)

    a = pair_ref[0] * _dot_nt(q.astype(BF16), key.astype(BF16))
    for lvl in range(N_LEVELS):
        dm = base - _block_boundary(incl, 1 << lvl)
        em = jnp.exp2(-jnp.abs(dm))
        a = a + pair_ref[lvl + 1] * _dot_nt((q * em).astype(BF16), (key * em).astype(BF16))

    tot = incl[CHUNK - 1:CHUNK, :]
    if rev:
        e_in = jnp.exp2(tot - base)
        e_out = jnp.exp2(base)
    else:
        e_in = jnp.exp2(incl)
        e_out = jnp.exp2(tot - incl)
    o = _dot(a.astype(BF16), vb) + _dot_nt((q * e_in).astype(BF16), st.astype(BF16))
    st_new = jnp.exp2(tot) * st + _dot_tn(vb, (key * e_out).astype(BF16))
    return o, st_new


def _hgrn_kernel(*refs, has_init, emit_state):
    refs = list(refs)
    q_ref, zf_ref, zb_ref, i_ref, g_ref, lb_ref, w_ref, pair_ref = refs[:8]
    refs = refs[8:]
    s0_ref = refs.pop(0) if has_init else None
    o_ref = refs.pop(0)
    s_out_ref = refs.pop(0) if emit_state else None
    of_ref, ob_ref, st_ref = refs

    n = q_ref.shape[0]
    nc = n // CHUNK
    hp = q_ref.shape[1] // HEAD_DIM
    lanes = [slice(hh * HEAD_DIM, (hh + 1) * HEAD_DIM) for hh in range(hp)]
    for d in range(2):
        for hh in range(hp):
            st_ref[d, hh] = (s0_ref[d, hh].T if has_init
                             else jnp.zeros((HEAD_DIM, HEAD_DIM), F32))

    def body(c, carry):
        rf = pl.ds(pl.multiple_of(c * CHUNK, CHUNK), CHUNK)
        rb = pl.ds(pl.multiple_of((nc - 1 - c) * CHUNK, CHUNK), CHUNK)
        scale = HEAD_DIM ** -0.5
        for hh, sl in enumerate(lanes):
            load = lambda ref, rows: ref[rows, sl].astype(F32)
            lb = lb_ref[hh]
            o, s = _hgrn_chunk(load(q_ref, rf) * scale, load(zf_ref, rf), load(i_ref, rf), lb,
                               pair_ref.at[0], st_ref[0, hh], False)
            of_ref[rf, sl] = o
            st_ref[0, hh] = s
            o, s = _hgrn_chunk(load(q_ref, rb) * scale, load(zb_ref, rb), load(i_ref, rb), lb,
                               pair_ref.at[1], st_ref[1, hh], True)
            ob_ref[rb, sl] = o
            st_ref[1, hh] = s
        return carry

    lax.fori_loop(0, nc, body, 0, unroll=min(nc, HGRN_UNROLL))
    for hh, sl in enumerate(lanes):
        o_ref[:, sl] = _gate_norm_out(of_ref[:, sl] + ob_ref[:, sl], g_ref[:, sl].astype(F32),
                                      w_ref[...])
        if emit_state:
            for d in range(2):
                s_out_ref[d, hh] = st_ref[d, hh].T


def hgrn_scan(proj, lb, norm_w, consts, seq_len, state_in, layer_idx, emit_state):
    r = proj.shape[0]
    b = r // seq_len
    has_init = state_in is not None
    hp = max(1, min(N_HEADS, HGRN_UNROLL // (seq_len // CHUNK)))
    nhb = N_HEADS // hp
    blk = (seq_len, hp * HEAD_DIM)
    col = lambda k: pl.BlockSpec(blk, lambda i, h, k=k: (i, k * nhb + h))
    state_blk = (None, None, 2, hp, HEAD_DIM, HEAD_DIM)
    state_spec = pl.BlockSpec(state_blk, lambda i, h: (i, layer_idx, 0, h, 0, 0))
    in_specs = [col(0), col(1), col(2), col(3), col(4),
                pl.BlockSpec((hp, 1, HEAD_DIM), lambda i, h: (h, 0, 0)),
                pl.BlockSpec((1, HEAD_DIM), lambda i, h: (0, 0)),
                pl.BlockSpec((2, N_LEVELS + 1, CHUNK, CHUNK), lambda i, h: (0, 0, 0, 0))]
    args = [proj, proj, proj, proj, proj, lb.reshape(N_HEADS, 1, HEAD_DIM),
            norm_w.reshape(1, HEAD_DIM), consts['pair']]
    if has_init:
        in_specs.append(state_spec)
        args.append(state_in)
    out_specs = [pl.BlockSpec(blk, lambda i, h: (i, h))]
    out_shape = [jax.ShapeDtypeStruct((r, D_MODEL), BF16)]
    if emit_state:
        out_specs.append(pl.BlockSpec(state_blk, lambda i, h: (i, 0, 0, h, 0, 0)))
        out_shape.append(jax.ShapeDtypeStruct((b, 1, 2, N_HEADS, HEAD_DIM, HEAD_DIM), F32))
    kern = functools.partial(_hgrn_kernel, has_init=has_init, emit_state=emit_state)
    res = pl.pallas_call(
        kern,
        grid=(b, nhb),
        in_specs=in_specs,
        out_specs=out_specs,
        out_shape=out_shape,
        scratch_shapes=[pltpu.VMEM(blk, F32), pltpu.VMEM(blk, F32),
                        pltpu.VMEM((2, hp, HEAD_DIM, HEAD_DIM), F32)],
        compiler_params=_params(("arbitrary", "arbitrary")),
        name="hgrn_scan",
    )(*args)
    return res if emit_state else (res[0], None)


def _lane_column(x, lane_idx):
    lane = lax.broadcasted_iota(jnp.int32, x.shape, 1)
    colv = jnp.sum(jnp.where(lane == lane_idx, x, 0.0), axis=1, keepdims=True)
    return jnp.broadcast_to(colv, x.shape)


def _gdn_prepare(items, block_ref):
    row = lax.broadcasted_iota(jnp.int32, (CHUNK, CHUNK), 0)
    colm = lax.broadcasted_iota(jnp.int32, (CHUNK, CHUNK), 1)
    pre = []
    for q, k, v, g, beta, kk, qk, rev in items:
        g_end = g[0:1, :] if rev else g[CHUNK - 1:CHUNK, :]
        within = (row <= colm) if rev else (row >= colm)
        strict = (row < colm) if rev else (row > colm)
        decay = jnp.exp2(jnp.where(within, g - g.T, -jnp.inf))
        a = jnp.where(strict, beta * kk * decay, 0.0)
        eg = jnp.exp2(g)
        x = jnp.concatenate([v * beta, k * (beta * eg)], axis=1).astype(BF16)
        p = (qk * decay).astype(BF16)
        ke = (k * jnp.exp2(g_end - g)).astype(BF16)
        pre.append((a, x, p, ke, q * eg, jnp.exp2(g_end)))
    ts = [block_ref[0] - it[0] * (block_ref[1] - block_ref[0]) for it in pre]
    for lvl in range(1, N_LEVELS):
        sel = block_ref[lvl + 1] - block_ref[lvl]
        tbs = [t.astype(BF16) for t in ts]
        lts = [_dot((it[0] * sel).astype(BF16), tb).astype(BF16) for it, tb in zip(pre, tbs)]
        ts = [t - _dot(tb, lt) for t, tb, lt in zip(ts, tbs, lts)]
    wus = [_dot(t.astype(BF16), it[1]).astype(BF16) for t, it in zip(ts, pre)]
    pwus = [_dot(it[2], wu) for it, wu in zip(pre, wus)]
    kwus = [_dot_tn(it[3], wu) for it, wu in zip(pre, wus)]
    out = []
    for it, pwu, kwu in zip(pre, pwus, kwus):
        qs = (it[4] - pwu[:, HEAD_DIM:]).astype(BF16)
        out.append((qs, pwu[:, :HEAD_DIM], kwu[:, HEAD_DIM:].astype(BF16), kwu[:, :HEAD_DIM], it[5]))
    return out


def _gdn_kernel(*refs, has_init, emit_state, hp):
    refs = list(refs)
    (q_ref, k_ref, v_ref, gate_ref, ab_ref, cq_ref, ck_ref, cv_ref, rate_ref, bias_ref,
     w_ref, block_ref) = refs[:12]
    refs = refs[12:]
    s0_ref = refs.pop(0) if has_init else None
    o_ref = refs.pop(0)
    s_out_ref = refs.pop(0) if emit_state else None
    qn_ref, kn_ref, vn_ref, qs_ref, ms_ref, o_acc_ref, b_ref, dec_ref, st_ref = refs

    n = q_ref.shape[0]
    nc = n // CHUNK
    group = max(1, min(nc, GDN_CHAINS // (2 * hp)))
    head0 = pl.program_id(1) * hp
    pos = lax.broadcasted_iota(jnp.int32, (n, hp * HEAD_DIM), 0)

    def conv_silu(x_ref, cw_ref):
        x = x_ref[...].astype(F32)
        prev = jnp.where(pos == 0, 0.0, pltpu.roll(x, 1, 0))
        nxt = jnp.where(pos == n - 1, 0.0, pltpu.roll(x, n - 1, 0))
        return _silu(prev * cw_ref[0:1, :] + x * cw_ref[1:2, :] + nxt * cw_ref[2:3, :])

    def l2norm(x):
        return x * lax.rsqrt(jnp.sum(x * x, axis=-1, keepdims=True) + 1e-6)

    qc = conv_silu(q_ref, cq_ref)
    kc = conv_silu(k_ref, ck_ref)
    vn_ref[...] = conv_silu(v_ref, cv_ref)
    for hh in range(hp):
        sl = slice(hh * HEAD_DIM, (hh + 1) * HEAD_DIM)
        qn_ref[:, sl] = l2norm(qc[:, sl]) * (HEAD_DIM ** -0.5)
        kn_ref[:, sl] = l2norm(kc[:, sl])
        for d in range(2):
            st_ref[d, hh] = s0_ref[d, hh] if has_init else jnp.zeros((HEAD_DIM, HEAD_DIM), F32)

    def chunk_rows(c):
        return pl.ds(pl.multiple_of(c * CHUNK, CHUNK), CHUNK)

    def prepare_body(i, carry):
        items, dest = [], []
        lane = lax.broadcasted_iota(jnp.int32, (CHUNK, HEAD_DIM), 1)
        for grp in range(group):
            rows = chunk_rows(i * group + grp)
            ab = ab_ref[rows, :]
            la = -(rate_ref[...] * LOG2E) * jax.nn.softplus(ab + bias_ref[...])
            sg_all = jax.nn.sigmoid(ab)
            incl = _prefix_rows(la)
            tot = incl[CHUNK - 1:CHUNK, :]
            g_all = jnp.where(lane >= N_HEADS, tot - incl + la, incl)
            for hh in range(hp):
                sl = slice(hh * HEAD_DIM, (hh + 1) * HEAD_DIM)
                q = qn_ref[rows, sl]
                k = kn_ref[rows, sl]
                v = vn_ref[rows, sl]
                kb = k.astype(BF16)
                kk = _dot_nt(kb, kb)
                qk = _dot_nt(q.astype(BF16), kb)
                for d in range(2):
                    g = _lane_column(g_all, d * N_HEADS + head0 + hh)
                    beta = _lane_column(sg_all, (2 + d) * N_HEADS + head0 + hh)
                    items.append((q, k, v, g, beta, kk, qk, d == 1))
                    dest.append((d, hh, rows))
        for (d, hh, rows), (qs, o0, ms, b, dec) in zip(dest, _gdn_prepare(items, block_ref)):
            qs_ref[d, hh, rows, :] = qs
            ms_ref[d, hh, rows, :] = ms
            o_acc_ref[d, hh, rows, :] = o0
            b_ref[d, hh, rows, :] = b
            dec_ref[d, hh, rows, :] = jnp.broadcast_to(dec, (CHUNK, HEAD_DIM))
        return carry

    lax.fori_loop(0, nc // group, prepare_body, 0)

    def scan_body(c, carry):
        for hh in range(hp):
            for d, cc in ((0, c), (1, nc - 1 - c)):
                rows = chunk_rows(cc)
                s = st_ref[d, hh]
                sb = s.astype(BF16)
                o_acc_ref[d, hh, rows, :] += _dot(qs_ref[d, hh, rows, :], sb)
                st_ref[d, hh] = (dec_ref[d, hh, rows, :] * s - _dot(ms_ref[d, hh, rows, :], sb)
                                 + b_ref[d, hh, rows, :])
        return carry

    lax.fori_loop(0, nc, scan_body, 0, unroll=2)
    for hh in range(hp):
        sl = slice(hh * HEAD_DIM, (hh + 1) * HEAD_DIM)
        o_ref[:, sl] = _gate_norm_out(o_acc_ref[0, hh] + o_acc_ref[1, hh],
                                      gate_ref[:, sl].astype(F32), w_ref[...])
    if emit_state:
        for d in range(2):
            for hh in range(hp):
                s_out_ref[d, hh] = st_ref[d, hh]


def gdn_scan(proj, ab, conv_w, rate, bias, norm_w, consts, seq_len, state_in, layer_idx,
             emit_state):
    r = proj.shape[0]
    b = r // seq_len
    has_init = state_in is not None
    nc = seq_len // CHUNK
    hp = GDN_HEADS_PER_STEP
    blk = (seq_len, hp * HEAD_DIM)
    nhb = N_HEADS // hp
    col = lambda k: pl.BlockSpec(blk, lambda i, h, k=k: (i, k * nhb + h))
    ccol = lambda k: pl.BlockSpec((3, hp * HEAD_DIM), lambda i, h, k=k: (0, k * nhb + h))
    vec = pl.BlockSpec((1, HEAD_DIM), lambda i, h: (0, 0))
    state_blk = (None, None, 2, hp, HEAD_DIM, HEAD_DIM)
    in_specs = [col(0), col(1), col(2), col(3),
                pl.BlockSpec((seq_len, HEAD_DIM), lambda i, h: (i, 0)),
                ccol(0), ccol(1), ccol(2), vec, vec, vec,
                pl.BlockSpec((N_LEVELS + 1, CHUNK, CHUNK), lambda i, h: (0, 0, 0))]
    args = [proj, proj, proj, proj, ab, conv_w, conv_w, conv_w, rate, bias,
            norm_w.reshape(1, HEAD_DIM), consts['block']]
    if has_init:
        in_specs.append(pl.BlockSpec(state_blk, lambda i, h: (i, layer_idx, 0, h, 0, 0)))
        args.append(state_in)
    out_specs = [pl.BlockSpec(blk, lambda i, h: (i, h))]
    out_shape = [jax.ShapeDtypeStruct((r, D_MODEL), BF16)]
    if emit_state:
        out_specs.append(pl.BlockSpec(state_blk, lambda i, h: (i, 0, 0, h, 0, 0)))
        out_shape.append(jax.ShapeDtypeStruct((b, 1, 2, N_HEADS, HEAD_DIM, HEAD_DIM), F32))
    kern = functools.partial(_gdn_kernel, has_init=has_init, emit_state=emit_state, hp=hp)
    per_head = (2, hp, seq_len, HEAD_DIM)
    res = pl.pallas_call(
        kern,
        grid=(b, nhb),
        in_specs=in_specs,
        out_specs=out_specs,
        out_shape=out_shape,
        scratch_shapes=[pltpu.VMEM(blk, F32)] * 3
        + [pltpu.VMEM(per_head, BF16)] * 2 + [pltpu.VMEM(per_head, F32)] * 3
        + [pltpu.VMEM((2, hp, HEAD_DIM, HEAD_DIM), F32)],
        compiler_params=_params(("arbitrary", "arbitrary")),
        name="gdn_scan",
    )(*args)
    return res if emit_state else (res[0], None)


def _rope_tables(n):
    rows = n // GRID_W
    row = jnp.repeat(jnp.arange(rows, dtype=F32), GRID_W)
    colp = jnp.tile(jnp.arange(GRID_W, dtype=F32), rows)
    nf = DA_HD // 4
    inv = ROPE_BASE ** (-jnp.arange(nf, dtype=F32) / nf)
    ar = row[:, None] * inv
    ac = colp[:, None] * inv
    cr, sr, cc, sc = jnp.cos(ar), jnp.sin(ar), jnp.cos(ac), jnp.sin(ac)
    cos = jnp.concatenate([cr, cr, cc, cc] * 2, axis=1)
    sin = jnp.concatenate([-sr, sr, -sc, sc] * 2, axis=1)
    return cos, sin


def _pad_halves(a, width, padded):
    zeros = jnp.zeros(a.shape[:-1] + (padded - width,), a.dtype)
    return jnp.concatenate([a[..., :width], zeros, a[..., width:], zeros], axis=-1)


def _trunk(x, mods, rows_per_mod, seq_len, P, cache):
    ctx_mode = cache is None
    depth = P['w_mod'].shape[0]
    n_attn = P['attn_w_in'].shape[0]
    kv, hgrn_state, gdn_state = None, None, None
    for i in range(depth):
        sh1, sc1, g1, sh2, sc2, g2 = mods[i]
        kind, j = i % 3, i // 3
        proj = functools.partial(norm_proj, x, P['norm_g'][i, 0], sc1, sh1,
                                 rows_per_mod=rows_per_mod)
        if kind == 0:
            qkv = proj(P['attn_w_in'][j], tn=1024, out_dtype=BF16)
            lam_init = 0.8 - 0.6 * math.exp(-0.3 * i)
            lp = P['attn_lambda'][j]
            lam = (jnp.exp(jnp.sum(lp[0] * lp[1])) - jnp.exp(jnp.sum(lp[2] * lp[3]))
                   + lam_init).reshape(1)
            if ctx_mode:
                o, kv = attn_context(qkv, lam, P['attn_subln'][j], seq_len, 1.0 - lam_init, j,
                                     n_attn, kv)
            else:
                o = attn_latent(qkv, cache['attn_k'], cache['attn_v'], j, cache['cos'],
                                cache['sin'], lam, P['attn_subln'][j], seq_len, 1.0 - lam_init)
            w_out = P['attn_w_out'][j]
        elif kind == 1:
            o, st = hgrn_scan(proj(P['hgrn_w_in'][j], tn=1024, out_dtype=BF16), P['lb_all'][i],
                              P['hgrn_norm'][j], P['consts'], seq_len,
                              None if ctx_mode else cache['hgrn'], j, ctx_mode)
            if ctx_mode:
                hgrn_state = st
            w_out = P['hgrn_w_out'][j]
        else:
            o, st = gdn_scan(proj(P['gdn_w_main'][j], tn=1024, out_dtype=BF16),
                             proj(P['gdn_w_tail'][j], tn=HEAD_DIM, out_dtype=F32),
                             P['gdn_conv'][j], P['gdn_rate'][j], P['gdn_bias'][j],
                             P['gdn_norm'][j], P['consts'], seq_len,
                             None if ctx_mode else cache['gdn'], j, ctx_mode)
            if ctx_mode:
                gdn_state = st
            w_out = P['gdn_w_out'][j]
        x = out_proj_residual(o, w_out, x, g1, rows_per_mod)
        x = conv_ffn_residual(x, P['norm_g'][i, 1], sc2, sh2, g2, P['ffn_w_up'][i],
                              P['ffn_conv'][i], P['ffn_conv_b'][i], P['ffn_w_down'][i],
                              P['final_g'], rows_per_mod, seq_len, i == depth - 1)
    return x, (kv, hgrn_state, gdn_state)


def kernel(x_prompt, x_sample, cache_attn_k, cache_attn_v, state_hgrn, state_gdn, c, c_ctx, norm_g, w_mod, b_mod, final_g, attn_w_in, attn_lambda, attn_subln, attn_w_out, hgrn_w_in, hgrn_lb, hgrn_norm, hgrn_w_out, gdn_w_in, gdn_conv, gdn_a_log, gdn_dt_bias, gdn_norm, gdn_w_out, ffn_w_up, ffn_conv, ffn_conv_b, ffn_w_down):
    batch, seq, d = x_prompt.shape
    dec_batch, dec_seq, _ = x_sample.shape
    depth = w_mod.shape[0]
    n_gdn = gdn_w_in.shape[0]

    lb_all = jnp.cumsum(jax.nn.softmax(hgrn_lb, axis=0), axis=0)
    lb_all = lb_all - lb_all[0]

    d_ff = ffn_w_down.shape[1]
    step_cols = FF_TILE * FFN_TILES_PER_STEP
    d_ff_pad = -(-d_ff // step_cols) * step_cols

    pad = jnp.zeros((n_gdn, HEAD_DIM - 2 * N_HEADS), F32)
    gdn_rate = jnp.concatenate([jnp.exp(gdn_a_log).reshape(n_gdn, 2 * N_HEADS), pad], axis=1)
    gdn_bias = jnp.concatenate([gdn_dt_bias.reshape(n_gdn, 2 * N_HEADS), pad], axis=1)
    tail = gdn_w_in[:, :, 4 * d:]
    tail = jnp.concatenate([tail, jnp.zeros((n_gdn, d, HEAD_DIM - tail.shape[2]), F32)], axis=2)

    P = dict(norm_g=norm_g, w_mod=w_mod, final_g=final_g,
             attn_w_in=attn_w_in.astype(BF16), attn_lambda=attn_lambda, attn_subln=attn_subln,
             attn_w_out=attn_w_out.astype(BF16),
             hgrn_w_in=hgrn_w_in.astype(BF16), lb_all=lb_all, hgrn_norm=hgrn_norm,
             hgrn_w_out=hgrn_w_out.astype(BF16),
             gdn_w_main=gdn_w_in[:, :, :4 * d].astype(BF16), gdn_w_tail=tail.astype(BF16),
             gdn_conv=gdn_conv, gdn_rate=gdn_rate.reshape(n_gdn, 1, HEAD_DIM),
             gdn_bias=gdn_bias.reshape(n_gdn, 1, HEAD_DIM), gdn_norm=gdn_norm,
             gdn_w_out=gdn_w_out.astype(BF16),
             ffn_w_up=_pad_halves(ffn_w_up.astype(BF16), d_ff, d_ff_pad),
             ffn_conv=_pad_halves(ffn_conv, d_ff, d_ff_pad),
             ffn_conv_b=_pad_halves(ffn_conv_b, d_ff, d_ff_pad),
             ffn_w_down=jnp.pad(ffn_w_down.astype(BF16), ((0, 0), (0, d_ff_pad - d_ff), (0, 0))),
             consts=_scan_constants())

    cond = jnp.zeros((16, d), F32).at[:dec_batch].set(c).at[dec_batch].set(c_ctx)
    mod = modulation_all(cond, w_mod, b_mod).reshape(depth, 16, 6, 1, d)
    mods_lat = [[mod[i, :dec_batch, t] for t in range(6)] for i in range(depth)]
    mods_ctx = [[mod[i, dec_batch:dec_batch + 1, t] for t in range(6)] for i in range(depth)]

    y_prompt, ((new_k, new_v), new_hgrn, new_gdn) = _trunk(
        x_prompt.reshape(batch * seq, d), mods_ctx, batch * seq, seq, P, None)

    cos, sin = _rope_tables(dec_seq)
    flat_k, flat_v = flatten_cache(cache_attn_k, cache_attn_v)
    cache = dict(attn_k=flat_k, attn_v=flat_v, hgrn=state_hgrn, gdn=state_gdn, cos=cos, sin=sin)
    y_sample, _ = _trunk(x_sample.reshape(dec_batch * dec_seq, d), mods_lat, dec_seq, dec_seq, P,
                         cache)

    return (y_prompt.reshape(batch, seq, d), y_sample.reshape(dec_batch, dec_seq, d),
            new_k, new_v, new_hgrn, new_gdn)
```

```python
import functools
import math

import jax
import jax.numpy as jnp
import numpy as np
from jax import lax
from jax.experimental import pallas as pl
from jax.experimental.pallas import tpu as pltpu

F32 = jnp.float32
BF16 = jnp.bfloat16

D_MODEL = 1024
N_HEADS = 8
HEAD_DIM = 128
DA_HD = 64
GRID_W = 64
ROPE_BASE = 10000.0
D_FF = 2816
EPS = 1e-6
LOG2E = 1.4426950408889634
CHUNK = 128
N_LEVELS = 7
ROW_TILE = 1024
FF_TILE = 256
FFN_TILES_PER_STEP = 6
Q_TILE = 256
SCORE_LOOKAHEAD = 8
HGRN_UNROLL = 8
GDN_CHAINS = 16
GDN_HEADS_PER_STEP = 4
VMEM_LIMIT = 56 * 1024 * 1024


def _params(sem, vmem=VMEM_LIMIT):
    return pltpu.CompilerParams(dimension_semantics=sem, vmem_limit_bytes=vmem)


def _dot(a, b):
    return jnp.dot(a, b, preferred_element_type=F32)


def _dot_nt(a, b):
    return lax.dot_general(a, b, (((1,), (1,)), ((), ())), preferred_element_type=F32)


def _dot_tn(a, b):
    return lax.dot_general(a, b, (((0,), (0,)), ((), ())), preferred_element_type=F32)


def _silu(x):
    return x * jax.nn.sigmoid(x)


def _norm_mod(x, g, sc, sh):
    y = x * lax.rsqrt(jnp.mean(x * x, axis=-1, keepdims=True) + EPS)
    return (y * g) * (1.0 + sc) + sh


def _mod_kernel(c_ref, w_ref, b_ref, o_ref):
    s = _silu(c_ref[...]).astype(BF16)
    o_ref[...] = _dot(s, w_ref[...].astype(BF16)) + b_ref[...]


def modulation_all(cond, w_mod, b_mod):
    depth, d, n = w_mod.shape
    tn = 1024
    return pl.pallas_call(
        _mod_kernel,
        grid=(depth, n // tn),
        in_specs=[pl.BlockSpec((16, d), lambda l, j: (0, 0)),
                  pl.BlockSpec((None, d, tn), lambda l, j: (l, 0, j)),
                  pl.BlockSpec((None, 1, tn), lambda l, j: (l, 0, j))],
        out_specs=pl.BlockSpec((None, 16, tn), lambda l, j: (l, 0, j)),
        out_shape=jax.ShapeDtypeStruct((depth, 16, n), F32),
        compiler_params=_params(("arbitrary", "arbitrary")),
        name="modulation",
    )(cond, w_mod, b_mod.reshape(depth, 1, n))


def _proj_kernel(x_ref, g_ref, sc_ref, sh_ref, w_ref, o_ref, h_ref):
    i = pl.program_id(1)

    @pl.when(pl.program_id(0) == 0)
    def _():
        h_ref[i] = _norm_mod(x_ref[...], g_ref[...], sc_ref[...], sh_ref[...]).astype(BF16)

    o_ref[...] = _dot(h_ref[i], w_ref[...]).astype(o_ref.dtype)


def norm_proj(x, g, sc, sh, w, rows_per_mod, tn, out_dtype):
    r, d = x.shape
    n = w.shape[1]
    tm = ROW_TILE
    first = lambda j, i: jnp.where(j == 0, i, 0)
    mod_spec = pl.BlockSpec((None, 1, d), lambda j, i: ((first(j, i) * tm) // rows_per_mod, 0, 0))
    return pl.pallas_call(
        _proj_kernel,
        grid=(n // tn, r // tm),
        in_specs=[pl.BlockSpec((tm, d), lambda j, i: (first(j, i), 0)),
                  pl.BlockSpec((1, d), lambda j, i: (0, 0)),
                  mod_spec, mod_spec,
                  pl.BlockSpec((d, tn), lambda j, i: (0, j))],
        out_specs=pl.BlockSpec((tm, tn), lambda j, i: (i, j)),
        out_shape=jax.ShapeDtypeStruct((r, n), out_dtype),
        scratch_shapes=[pltpu.VMEM((r // tm, tm, d), BF16)],
        compiler_params=_params(("arbitrary", "arbitrary")),
        name="norm_proj",
    )(x, g.reshape(1, d), sc, sh, w)


def _out_proj_kernel(o_ref, w_ref, x_ref, gate_ref, y_ref):
    y_ref[...] = x_ref[...] + gate_ref[...] * _dot(o_ref[...], w_ref[...])


def out_proj_residual(o, w, x, gate, rows_per_mod):
    r, d = x.shape
    tm = ROW_TILE
    return pl.pallas_call(
        _out_proj_kernel,
        grid=(r // tm,),
        in_specs=[pl.BlockSpec((tm, d), lambda i: (i, 0)),
                  pl.BlockSpec((d, d), lambda i: (0, 0)),
                  pl.BlockSpec((tm, d), lambda i: (i, 0)),
                  pl.BlockSpec((None, 1, d), lambda i: ((i * tm) // rows_per_mod, 0, 0))],
        out_specs=pl.BlockSpec((tm, d), lambda i: (i, 0)),
        out_shape=jax.ShapeDtypeStruct((r, d), F32),
        compiler_params=_params(("arbitrary",)),
        name="out_proj",
    )(o, w, x, gate)


def _ffn_kernel(*refs, seq_len, final_norm):
    x_ref, g_ref, sc_ref, sh_ref, gate_ref = refs[:5]
    nt = FFN_TILES_PER_STEP
    wv_refs, wg_refs, cv_refs, cg_refs, bv_refs, bg_refs, wd_refs = (
        refs[5 + k * nt:5 + (k + 1) * nt] for k in range(7))
    fg_ref, y_ref, h_ref, acc_ref = refs[5 + 7 * nt:]
    j = pl.program_id(1)

    @pl.when(j == 0)
    def _():
        h_ref[...] = _norm_mod(x_ref[...], g_ref[...], sc_ref[...], sh_ref[...]).astype(BF16)
        acc_ref[...] = jnp.zeros_like(acc_ref)

    tm = h_ref.shape[0]
    h = h_ref[...]
    pos = lax.broadcasted_iota(jnp.int32, (tm, FF_TILE), 0) & (seq_len - 1)
    first = pos == 0
    last = pos == seq_len - 1

    def conv(u, cw_ref, b_ref):
        prev = jnp.where(first, 0.0, pltpu.roll(u, 1, 0))
        nxt = jnp.where(last, 0.0, pltpu.roll(u, tm - 1, 0))
        return prev * cw_ref[0:1, :] + u * cw_ref[1:2, :] + nxt * cw_ref[2:3, :] + b_ref[...]

    ups = [(_dot(h, wv_refs[t][...]), _dot(h, wg_refs[t][...])) for t in range(nt)]
    for t in range(nt):
        val = conv(ups[t][0], cv_refs[t], bv_refs[t])
        gte = conv(ups[t][1], cg_refs[t], bg_refs[t])
        act = (val * _silu(gte)).astype(BF16)
        acc_ref[...] += _dot(act, wd_refs[t][...])

    @pl.when(j == pl.num_programs(1) - 1)
    def _():
        y = x_ref[...] + gate_ref[...] * acc_ref[...]
        if final_norm:
            y = y * lax.rsqrt(jnp.mean(y * y, axis=-1, keepdims=True) + EPS) * fg_ref[...]
        y_ref[...] = y


def conv_ffn_residual(x, g, sc, sh, gate, w_up, conv_w, conv_b, w_down, final_g, rows_per_mod,
                      seq_len, final_norm):
    r, d = x.shape
    tm = ROW_TILE
    nt = FFN_TILES_PER_STEP
    nf = w_down.shape[0] // FF_TILE
    mod_spec = pl.BlockSpec((None, 1, d), lambda i, j: ((i * tm) // rows_per_mod, 0, 0))
    kern = functools.partial(_ffn_kernel, seq_len=seq_len, final_norm=final_norm)
    conv_b = conv_b.reshape(1, -1)

    def cols(rows, half):
        return [pl.BlockSpec((rows, FF_TILE), lambda i, j, t=t: (0, half * nf + nt * j + t))
                for t in range(nt)]

    in_specs = ([pl.BlockSpec((tm, d), lambda i, j: (i, 0)),
                 pl.BlockSpec((1, d), lambda i, j: (0, 0)),
                 mod_spec, mod_spec, mod_spec]
                + cols(d, 0) + cols(d, 1) + cols(3, 0) + cols(3, 1) + cols(1, 0) + cols(1, 1)
                + [pl.BlockSpec((FF_TILE, d), lambda i, j, t=t: (nt * j + t, 0)) for t in range(nt)]
                + [pl.BlockSpec((1, d), lambda i, j: (0, 0))])
    args = ([x, g.reshape(1, d), sc, sh, gate] + [w_up] * (2 * nt) + [conv_w] * (2 * nt)
            + [conv_b] * (2 * nt) + [w_down] * nt + [final_g.reshape(1, d)])
    return pl.pallas_call(
        kern,
        grid=(r // tm, nf // nt),
        in_specs=in_specs,
        out_specs=pl.BlockSpec((tm, d), lambda i, j: (i, 0)),
        out_shape=jax.ShapeDtypeStruct((r, d), F32),
        scratch_shapes=[pltpu.VMEM((tm, d), BF16), pltpu.VMEM((tm, d), F32)],
        compiler_params=_params(("arbitrary", "arbitrary")),
        name="conv_ffn",
    )(*args)


def _ones_column_block(rows):
    lane = lax.broadcasted_iota(jnp.int32, (rows, HEAD_DIM), 1)
    return jnp.where(lane == 0, 1.0, 0.0).astype(BF16)


def _diff_attend(tiles, lam, emit):
    chains = [(t, c) for t in range(len(tiles)) for c in range(2)]
    q_cache = {}

    def scores(t, c):
        if t not in q_cache:
            q_cache[t] = tiles[t][0]()
        q = q_cache[t]
        lane = lax.broadcasted_iota(jnp.int32, q.shape, 1)
        qc = jnp.where((lane < DA_HD) == (c == 0), q, 0.0).astype(BF16)
        return _dot_nt(qc, tiles[t][1])

    ahead = SCORE_LOOKAHEAD
    pending = [scores(*ch) for ch in chains[:ahead]]
    first = None
    for idx, (t, c) in enumerate(chains):
        if idx + ahead < len(chains):
            pending.append(scores(*chains[idx + ahead]))
        s = pending.pop(0)
        e = jnp.exp2(s - jnp.max(s, axis=-1, keepdims=True)).astype(BF16)
        pv = _dot(e, tiles[t][2])
        o = pv[:, :HEAD_DIM] / pv[:, HEAD_DIM:HEAD_DIM + 1]
        if c == 0:
            first = o
        else:
            emit(t, first - lam * o)


def _subln(o, w, post_scale):
    return o * lax.rsqrt(jnp.mean(o * o, axis=-1, keepdims=True) + 1e-5) * w * post_scale


def _attn_ctx_kernel(*refs, post_scale, layer_idx, first_layer):
    if first_layer:
        lam_ref, q_ref, k_ref, v_ref, w_ref, o_ref, nk_ref, nv_ref = refs
    else:
        lam_ref, q_ref, k_ref, v_ref, w_ref, _, _, o_ref, nk_ref, nv_ref = refs
    lam = lam_ref[0]
    n = q_ref.shape[0]
    ones = _ones_column_block(n)
    scale = DA_HD ** -0.5 * LOG2E
    for h in range(N_HEADS):
        sl = slice(h * HEAD_DIM, (h + 1) * HEAD_DIM)
        v1 = jnp.concatenate([v_ref[:, sl], ones], axis=1)

        def emit(t, o, sl=sl):
            o_ref[:, sl] = _subln(o, w_ref[...], post_scale).astype(BF16)

        _diff_attend([(lambda sl=sl: q_ref[:, sl].astype(F32) * scale, k_ref[:, sl], v1)], lam,
                     emit)

    kv_shape = (n, N_HEADS, HEAD_DIM)
    if first_layer:
        for l in range(nk_ref.shape[0]):
            if l == layer_idx:
                nk_ref[l] = k_ref[...].astype(F32).reshape(kv_shape)
                nv_ref[l] = v_ref[...].astype(F32).reshape(kv_shape)
            else:
                nk_ref[l] = jnp.zeros(kv_shape, F32)
                nv_ref[l] = jnp.zeros(kv_shape, F32)
    else:
        nk_ref[...] = k_ref[...].astype(F32).reshape(kv_shape)
        nv_ref[...] = v_ref[...].astype(F32).reshape(kv_shape)


def attn_context(qkv, lam, subln_w, seq_len, post_scale, layer_idx, n_layers, kv_prev):
    r = qkv.shape[0]
    b = r // seq_len
    d = D_MODEL
    first_layer = kv_prev is None
    kern = functools.partial(_attn_ctx_kernel, post_scale=post_scale, layer_idx=layer_idx,
                             first_layer=first_layer)
    in_specs = [pl.BlockSpec(memory_space=pltpu.SMEM),
                pl.BlockSpec((seq_len, d), lambda i: (i, 0)),
                pl.BlockSpec((seq_len, d), lambda i: (i, 1)),
                pl.BlockSpec((seq_len, d), lambda i: (i, 2)),
                pl.BlockSpec((1, HEAD_DIM), lambda i: (0, 0))]
    args = [lam, qkv, qkv, qkv, subln_w.reshape(1, HEAD_DIM)]
    kv_sds = jax.ShapeDtypeStruct((b, n_layers, seq_len, N_HEADS, HEAD_DIM), F32)
    if first_layer:
        kv_spec = pl.BlockSpec((None, n_layers, seq_len, N_HEADS, HEAD_DIM),
                               lambda i: (i, 0, 0, 0, 0))
        aliases = {}
    else:
        kv_spec = pl.BlockSpec((None, None, seq_len, N_HEADS, HEAD_DIM),
                               lambda i: (i, layer_idx, 0, 0, 0))
        in_specs += [pl.BlockSpec(memory_space=pl.ANY)] * 2
        args += list(kv_prev)
        aliases = {5: 1, 6: 2}
    o, nk, nv = pl.pallas_call(
        kern,
        grid=(b,),
        in_specs=in_specs,
        out_specs=[pl.BlockSpec((seq_len, d), lambda i: (i, 0)), kv_spec, kv_spec],
        out_shape=[jax.ShapeDtypeStruct((r, d), BF16), kv_sds, kv_sds],
        input_output_aliases=aliases,
        compiler_params=_params(("arbitrary",)),
        name="attn_context",
    )(*args)
    return o, (nk, nv)


def _rope(x, cos, sin):
    lane = lax.broadcasted_iota(jnp.int32, x.shape, 1)
    partner = jnp.where((lane & 16) == 0, pltpu.roll(x, HEAD_DIM - 16, 1), pltpu.roll(x, 16, 1))
    return x * cos + partner * sin


def _attn_lat_kernel(lam_ref, q_ref, k_ref, v_ref, ck_ref, cv_ref, cos_ref, sin_ref, w_ref, o_ref,
                     kcat_ref, vcat_ref, *, post_scale):
    lam = lam_ref[0]
    past = ck_ref.shape[0]
    n = q_ref.shape[0]
    kcat_ref[0:past, :] = ck_ref[...]
    kcat_ref[past:past + n, :] = _rope(k_ref[...].astype(F32), cos_ref[...],
                                       sin_ref[...]).astype(BF16)
    vcat_ref[0:past, 0:HEAD_DIM] = cv_ref[...]
    vcat_ref[past:past + n, 0:HEAD_DIM] = v_ref[...]
    vcat_ref[:, HEAD_DIM:] = _ones_column_block(past + n)
    scale = DA_HD ** -0.5 * LOG2E

    def q_tile(t):
        rows = slice(t * Q_TILE, (t + 1) * Q_TILE)
        return _rope(q_ref[rows, :].astype(F32), cos_ref[rows, :], sin_ref[rows, :]) * scale

    def emit(t, o):
        o_ref[t * Q_TILE:(t + 1) * Q_TILE, :] = _subln(o, w_ref[...], post_scale).astype(BF16)

    k = kcat_ref[...]
    v1 = vcat_ref[...]
    _diff_attend([(functools.partial(q_tile, t), k, v1) for t in range(n // Q_TILE)], lam, emit)


def _flatten_heads_kernel(k_ref, v_ref, ok_ref, ov_ref):
    shape = ok_ref.shape
    ok_ref[...] = k_ref[...].reshape(shape).astype(BF16)
    ov_ref[...] = v_ref[...].reshape(shape).astype(BF16)


def flatten_cache(cache_k, cache_v):
    b, l, past, h, e = cache_k.shape
    in_spec = pl.BlockSpec((None, None, past, h, e), lambda i, j: (i, j, 0, 0, 0))
    out_spec = pl.BlockSpec((None, None, past, h * e), lambda i, j: (i, j, 0, 0))
    sds = jax.ShapeDtypeStruct((b, l, past, h * e), BF16)
    return pl.pallas_call(
        _flatten_heads_kernel,
        grid=(b, l),
        in_specs=[in_spec, in_spec],
        out_specs=[out_spec, out_spec],
        out_shape=[sds, sds],
        compiler_params=_params(("arbitrary", "arbitrary")),
        name="flatten_cache",
    )(cache_k, cache_v)


def attn_latent(qkv, cache_k, cache_v, layer_idx, cos, sin, lam, subln_w, seq_len, post_scale):
    r = qkv.shape[0]
    b = r // seq_len
    past = cache_k.shape[2]
    kern = functools.partial(_attn_lat_kernel, post_scale=post_scale)
    blk = (seq_len, HEAD_DIM)
    cache_spec = pl.BlockSpec((None, None, past, HEAD_DIM), lambda i, h: (i, layer_idx, 0, h))
    tab_spec = pl.BlockSpec((seq_len, HEAD_DIM), lambda i, h: (0, 0))
    return pl.pallas_call(
        kern,
        grid=(b, N_HEADS),
        in_specs=[pl.BlockSpec(memory_space=pltpu.SMEM),
                  pl.BlockSpec(blk, lambda i, h: (i, h)),
                  pl.BlockSpec(blk, lambda i, h: (i, N_HEADS + h)),
                  pl.BlockSpec(blk, lambda i, h: (i, 2 * N_HEADS + h)),
                  cache_spec, cache_spec, tab_spec, tab_spec,
                  pl.BlockSpec((1, HEAD_DIM), lambda i, h: (0, 0))],
        out_specs=pl.BlockSpec(blk, lambda i, h: (i, h)),
        out_shape=jax.ShapeDtypeStruct((r, D_MODEL), BF16),
        scratch_shapes=[pltpu.VMEM((past + seq_len, HEAD_DIM), BF16),
                        pltpu.VMEM((past + seq_len, 2 * HEAD_DIM), BF16)],
        compiler_params=_params(("arbitrary", "arbitrary")),
        name="attn_latent",
    )(lam, qkv, qkv, qkv, cache_k, cache_v, cos, sin, subln_w.reshape(1, HEAD_DIM))


def _prefix_rows(x):
    row = lax.broadcasted_iota(jnp.int32, x.shape, 0)
    for j in range(N_LEVELS):
        s = 1 << j
        x = x + jnp.where(row >= s, pltpu.roll(x, s, 0), 0.0)
    return x


def _block_boundary(x, m):
    n, lanes = x.shape
    w = 2 * m
    if w >= 8:
        y = x.reshape(n // w, w, lanes)
        return jnp.broadcast_to(y[:, m - 1:m, :], y.shape).reshape(n, lanes)
    y = x.reshape(n // 8, 8, lanes)
    sub = lax.broadcasted_iota(jnp.int32, y.shape, 1)
    out = None
    for grp in range(8 // w):
        src = grp * w + m - 1
        b = jnp.broadcast_to(y[:, src:src + 1, :], y.shape)
        out = b if out is None else jnp.where(sub >= grp * w, b, out)
    return out.reshape(n, lanes)


def _scan_constants():
    t = np.arange(CHUNK)
    block = [(t[:, None] >> j) == (t[None, :] >> j) for j in range(N_LEVELS + 1)]
    fwd = [block[0]]
    for j in range(N_LEVELS):
        m = 1 << j
        up = (t & m) != 0
        fwd.append(block[j + 1] & up[:, None] & ~up[None, :])
    pair = np.stack([np.stack(fwd), np.stack([a.T for a in fwd])])
    return dict(block=jnp.asarray(np.stack(block).astype(np.float32)),
                pair=jnp.asarray(pair.astype(np.float32)))


def _gate_norm_out(o, gate, w):
    y = o * lax.rsqrt(jnp.mean(o * o, axis=-1, keepdims=True) + EPS) * w
    return (y * _silu(gate)).astype(BF16)


def _hgrn_chunk(q, z, v, lb, pair_ref, st, rev):
    e = jnp.exp(-jnp.abs(z))
    r = 1.0 / (1.0 + e)
    pos = z >= 0
    sig = jnp.where(pos, r, e * r)
    nsig = jnp.where(pos, e * r, r)
    f = lb + (1.0 - lb) * sig
    key = (1.0 - lb) * nsig
    lf = jnp.log2(f)
    incl = _prefix_rows(lf)
    base = incl - lf if rev else incl
    vb = v.astype(BF16)

    a = pair_ref[0] * _dot_nt(q.astype(BF16), key.astype(BF16))
    for lvl in range(N_LEVELS):
        dm = base - _block_boundary(incl, 1 << lvl)
        em = jnp.exp2(-jnp.abs(dm))
        a = a + pair_ref[lvl + 1] * _dot_nt((q * em).astype(BF16), (key * em).astype(BF16))

    tot = incl[CHUNK - 1:CHUNK, :]
    if rev:
        e_in = jnp.exp2(tot - base)
        e_out = jnp.exp2(base)
    else:
        e_in = jnp.exp2(incl)
        e_out = jnp.exp2(tot - incl)
    o = _dot(a.astype(BF16), vb) + _dot_nt((q * e_in).astype(BF16), st.astype(BF16))
    st_new = jnp.exp2(tot) * st + _dot_tn(vb, (key * e_out).astype(BF16))
    return o, st_new


def _hgrn_kernel(*refs, has_init, emit_state):
    refs = list(refs)
    q_ref, zf_ref, zb_ref, i_ref, g_ref, lb_ref, w_ref, pair_ref = refs[:8]
    refs = refs[8:]
    s0_ref = refs.pop(0) if has_init else None
    o_ref = refs.pop(0)
    s_out_ref = refs.pop(0) if emit_state else None
    of_ref, ob_ref, st_ref = refs

    n = q_ref.shape[0]
    nc = n // CHUNK
    hp = q_ref.shape[1] // HEAD_DIM
    lanes = [slice(hh * HEAD_DIM, (hh + 1) * HEAD_DIM) for hh in range(hp)]
    for d in range(2):
        for hh in range(hp):
            st_ref[d, hh] = (s0_ref[d, hh].T if has_init
                             else jnp.zeros((HEAD_DIM, HEAD_DIM), F32))

    def body(c, carry):
        rf = pl.ds(pl.multiple_of(c * CHUNK, CHUNK), CHUNK)
        rb = pl.ds(pl.multiple_of((nc - 1 - c) * CHUNK, CHUNK), CHUNK)
        scale = HEAD_DIM ** -0.5
        for hh, sl in enumerate(lanes):
            load = lambda ref, rows: ref[rows, sl].astype(F32)
            lb = lb_ref[hh]
            o, s = _hgrn_chunk(load(q_ref, rf) * scale, load(zf_ref, rf), load(i_ref, rf), lb,
                               pair_ref.at[0], st_ref[0, hh], False)
            of_ref[rf, sl] = o
            st_ref[0, hh] = s
            o, s = _hgrn_chunk(load(q_ref, rb) * scale, load(zb_ref, rb), load(i_ref, rb), lb,
                               pair_ref.at[1], st_ref[1, hh], True)
            ob_ref[rb, sl] = o
            st_ref[1, hh] = s
        return carry

    lax.fori_loop(0, nc, body, 0, unroll=min(nc, HGRN_UNROLL))
    for hh, sl in enumerate(lanes):
        o_ref[:, sl] = _gate_norm_out(of_ref[:, sl] + ob_ref[:, sl], g_ref[:, sl].astype(F32),
                                      w_ref[...])
        if emit_state:
            for d in range(2):
                s_out_ref[d, hh] = st_ref[d, hh].T


def hgrn_scan(proj, lb, norm_w, consts, seq_len, state_in, layer_idx, emit_state):
    r = proj.shape[0]
    b = r // seq_len
    has_init = state_in is not None
    hp = max(1, min(N_HEADS, HGRN_UNROLL // (seq_len // CHUNK)))
    nhb = N_HEADS // hp
    blk = (seq_len, hp * HEAD_DIM)
    col = lambda k: pl.BlockSpec(blk, lambda i, h, k=k: (i, k * nhb + h))
    state_blk = (None, None, 2, hp, HEAD_DIM, HEAD_DIM)
    state_spec = pl.BlockSpec(state_blk, lambda i, h: (i, layer_idx, 0, h, 0, 0))
    in_specs = [col(0), col(1), col(2), col(3), col(4),
                pl.BlockSpec((hp, 1, HEAD_DIM), lambda i, h: (h, 0, 0)),
                pl.BlockSpec((1, HEAD_DIM), lambda i, h: (0, 0)),
                pl.BlockSpec((2, N_LEVELS + 1, CHUNK, CHUNK), lambda i, h: (0, 0, 0, 0))]
    args = [proj, proj, proj, proj, proj, lb.reshape(N_HEADS, 1, HEAD_DIM),
            norm_w.reshape(1, HEAD_DIM), consts['pair']]
    if has_init:
        in_specs.append(state_spec)
        args.append(state_in)
    out_specs = [pl.BlockSpec(blk, lambda i, h: (i, h))]
    out_shape = [jax.ShapeDtypeStruct((r, D_MODEL), BF16)]
    if emit_state:
        out_specs.append(pl.BlockSpec(state_blk, lambda i, h: (i, 0, 0, h, 0, 0)))
        out_shape.append(jax.ShapeDtypeStruct((b, 1, 2, N_HEADS, HEAD_DIM, HEAD_DIM), F32))
    kern = functools.partial(_hgrn_kernel, has_init=has_init, emit_state=emit_state)
    res = pl.pallas_call(
        kern,
        grid=(b, nhb),
        in_specs=in_specs,
        out_specs=out_specs,
        out_shape=out_shape,
        scratch_shapes=[pltpu.VMEM(blk, F32), pltpu.VMEM(blk, F32),
                        pltpu.VMEM((2, hp, HEAD_DIM, HEAD_DIM), F32)],
        compiler_params=_params(("arbitrary", "arbitrary")),
        name="hgrn_scan",
    )(*args)
    return res if emit_state else (res[0], None)


def _lane_column(x, lane_idx):
    lane = lax.broadcasted_iota(jnp.int32, x.shape, 1)
    colv = jnp.sum(jnp.where(lane == lane_idx, x, 0.0), axis=1, keepdims=True)
    return jnp.broadcast_to(colv, x.shape)


def _gdn_prepare(items, block_ref):
    row = lax.broadcasted_iota(jnp.int32, (CHUNK, CHUNK), 0)
    colm = lax.broadcasted_iota(jnp.int32, (CHUNK, CHUNK), 1)
    pre = []
    for q, k, v, g, beta, kk, qk, rev in items:
        g_end = g[0:1, :] if rev else g[CHUNK - 1:CHUNK, :]
        within = (row <= colm) if rev else (row >= colm)
        strict = (row < colm) if rev else (row > colm)
        decay = jnp.exp2(jnp.where(within, g - g.T, -jnp.inf))
        a = jnp.where(strict, beta * kk * decay, 0.0)
        eg = jnp.exp2(g)
        x = jnp.concatenate([v * beta, k * (beta * eg)], axis=1).astype(BF16)
        p = (qk * decay).astype(BF16)
        ke = (k * jnp.exp2(g_end - g)).astype(BF16)
        pre.append((a, x, p, ke, q * eg, jnp.exp2(g_end)))
    ts = [block_ref[0] - it[0] * (block_ref[1] - block_ref[0]) for it in pre]
    for lvl in range(1, N_LEVELS):
        sel = block_ref[lvl + 1] - block_ref[lvl]
        tbs = [t.astype(BF16) for t in ts]
        lts = [_dot((it[0] * sel).astype(BF16), tb).astype(BF16) for it, tb in zip(pre, tbs)]
        ts = [t - _dot(tb, lt) for t, tb, lt in zip(ts, tbs, lts)]
    wus = [_dot(t.astype(BF16), it[1]).astype(BF16) for t, it in zip(ts, pre)]
    pwus = [_dot(it[2], wu) for it, wu in zip(pre, wus)]
    kwus = [_dot_tn(it[3], wu) for it, wu in zip(pre, wus)]
    out = []
    for it, pwu, kwu in zip(pre, pwus, kwus):
        qs = (it[4] - pwu[:, HEAD_DIM:]).astype(BF16)
        out.append((qs, pwu[:, :HEAD_DIM], kwu[:, HEAD_DIM:].astype(BF16), kwu[:, :HEAD_DIM], it[5]))
    return out


def _gdn_kernel(*refs, has_init, emit_state, hp):
    refs = list(refs)
    (q_ref, k_ref, v_ref, gate_ref, ab_ref, cq_ref, ck_ref, cv_ref, rate_ref, bias_ref,
     w_ref, block_ref) = refs[:12]
    refs = refs[12:]
    s0_ref = refs.pop(0) if has_init else None
    o_ref = refs.pop(0)
    s_out_ref = refs.pop(0) if emit_state else None
    qn_ref, kn_ref, vn_ref, qs_ref, ms_ref, o_acc_ref, b_ref, dec_ref, st_ref = refs

    n = q_ref.shape[0]
    nc = n // CHUNK
    group = max(1, min(nc, GDN_CHAINS // (2 * hp)))
    head0 = pl.program_id(1) * hp
    pos = lax.broadcasted_iota(jnp.int32, (n, hp * HEAD_DIM), 0)

    def conv_silu(x_ref, cw_ref):
        x = x_ref[...].astype(F32)
        prev = jnp.where(pos == 0, 0.0, pltpu.roll(x, 1, 0))
        nxt = jnp.where(pos == n - 1, 0.0, pltpu.roll(x, n - 1, 0))
        return _silu(prev * cw_ref[0:1, :] + x * cw_ref[1:2, :] + nxt * cw_ref[2:3, :])

    def l2norm(x):
        return x * lax.rsqrt(jnp.sum(x * x, axis=-1, keepdims=True) + 1e-6)

    qc = conv_silu(q_ref, cq_ref)
    kc = conv_silu(k_ref, ck_ref)
    vn_ref[...] = conv_silu(v_ref, cv_ref)
    for hh in range(hp):
        sl = slice(hh * HEAD_DIM, (hh + 1) * HEAD_DIM)
        qn_ref[:, sl] = l2norm(qc[:, sl]) * (HEAD_DIM ** -0.5)
        kn_ref[:, sl] = l2norm(kc[:, sl])
        for d in range(2):
            st_ref[d, hh] = s0_ref[d, hh] if has_init else jnp.zeros((HEAD_DIM, HEAD_DIM), F32)

    def chunk_rows(c):
        return pl.ds(pl.multiple_of(c * CHUNK, CHUNK), CHUNK)

    def prepare_body(i, carry):
        items, dest = [], []
        lane = lax.broadcasted_iota(jnp.int32, (CHUNK, HEAD_DIM), 1)
        for grp in range(group):
            rows = chunk_rows(i * group + grp)
            ab = ab_ref[rows, :]
            la = -(rate_ref[...] * LOG2E) * jax.nn.softplus(ab + bias_ref[...])
            sg_all = jax.nn.sigmoid(ab)
            incl = _prefix_rows(la)
            tot = incl[CHUNK - 1:CHUNK, :]
            g_all = jnp.where(lane >= N_HEADS, tot - incl + la, incl)
            for hh in range(hp):
                sl = slice(hh * HEAD_DIM, (hh + 1) * HEAD_DIM)
                q = qn_ref[rows, sl]
                k = kn_ref[rows, sl]
                v = vn_ref[rows, sl]
                kb = k.astype(BF16)
                kk = _dot_nt(kb, kb)
                qk = _dot_nt(q.astype(BF16), kb)
                for d in range(2):
                    g = _lane_column(g_all, d * N_HEADS + head0 + hh)
                    beta = _lane_column(sg_all, (2 + d) * N_HEADS + head0 + hh)
                    items.append((q, k, v, g, beta, kk, qk, d == 1))
                    dest.append((d, hh, rows))
        for (d, hh, rows), (qs, o0, ms, b, dec) in zip(dest, _gdn_prepare(items, block_ref)):
            qs_ref[d, hh, rows, :] = qs
            ms_ref[d, hh, rows, :] = ms
            o_acc_ref[d, hh, rows, :] = o0
            b_ref[d, hh, rows, :] = b
            dec_ref[d, hh, rows, :] = jnp.broadcast_to(dec, (CHUNK, HEAD_DIM))
        return carry

    lax.fori_loop(0, nc // group, prepare_body, 0)

    def scan_body(c, carry):
        for hh in range(hp):
            for d, cc in ((0, c), (1, nc - 1 - c)):
                rows = chunk_rows(cc)
                s = st_ref[d, hh]
                sb = s.astype(BF16)
                o_acc_ref[d, hh, rows, :] += _dot(qs_ref[d, hh, rows, :], sb)
                st_ref[d, hh] = (dec_ref[d, hh, rows, :] * s - _dot(ms_ref[d, hh, rows, :], sb)
                                 + b_ref[d, hh, rows, :])
        return carry

    lax.fori_loop(0, nc, scan_body, 0, unroll=2)
    for hh in range(hp):
        sl = slice(hh * HEAD_DIM, (hh + 1) * HEAD_DIM)
        o_ref[:, sl] = _gate_norm_out(o_acc_ref[0, hh] + o_acc_ref[1, hh],
                                      gate_ref[:, sl].astype(F32), w_ref[...])
    if emit_state:
        for d in range(2):
            for hh in range(hp):
                s_out_ref[d, hh] = st_ref[d, hh]


def gdn_scan(proj, ab, conv_w, rate, bias, norm_w, consts, seq_len, state_in, layer_idx,
             emit_state):
    r = proj.shape[0]
    b = r // seq_len
    has_init = state_in is not None
    nc = seq_len // CHUNK
    hp = GDN_HEADS_PER_STEP
    blk = (seq_len, hp * HEAD_DIM)
    nhb = N_HEADS // hp
    col = lambda k: pl.BlockSpec(blk, lambda i, h, k=k: (i, k * nhb + h))
    ccol = lambda k: pl.BlockSpec((3, hp * HEAD_DIM), lambda i, h, k=k: (0, k * nhb + h))
    vec = pl.BlockSpec((1, HEAD_DIM), lambda i, h: (0, 0))
    state_blk = (None, None, 2, hp, HEAD_DIM, HEAD_DIM)
    in_specs = [col(0), col(1), col(2), col(3),
                pl.BlockSpec((seq_len, HEAD_DIM), lambda i, h: (i, 0)),
                ccol(0), ccol(1), ccol(2), vec, vec, vec,
                pl.BlockSpec((N_LEVELS + 1, CHUNK, CHUNK), lambda i, h: (0, 0, 0))]
    args = [proj, proj, proj, proj, ab, conv_w, conv_w, conv_w, rate, bias,
            norm_w.reshape(1, HEAD_DIM), consts['block']]
    if has_init:
        in_specs.append(pl.BlockSpec(state_blk, lambda i, h: (i, layer_idx, 0, h, 0, 0)))
        args.append(state_in)
    out_specs = [pl.BlockSpec(blk, lambda i, h: (i, h))]
    out_shape = [jax.ShapeDtypeStruct((r, D_MODEL), BF16)]
    if emit_state:
        out_specs.append(pl.BlockSpec(state_blk, lambda i, h: (i, 0, 0, h, 0, 0)))
        out_shape.append(jax.ShapeDtypeStruct((b, 1, 2, N_HEADS, HEAD_DIM, HEAD_DIM), F32))
    kern = functools.partial(_gdn_kernel, has_init=has_init, emit_state=emit_state, hp=hp)
    per_head = (2, hp, seq_len, HEAD_DIM)
    res = pl.pallas_call(
        kern,
        grid=(b, nhb),
        in_specs=in_specs,
        out_specs=out_specs,
        out_shape=out_shape,
        scratch_shapes=[pltpu.VMEM(blk, F32)] * 3
        + [pltpu.VMEM(per_head, BF16)] * 2 + [pltpu.VMEM(per_head, F32)] * 3
        + [pltpu.VMEM((2, hp, HEAD_DIM, HEAD_DIM), F32)],
        compiler_params=_params(("arbitrary", "arbitrary")),
        name="gdn_scan",
    )(*args)
    return res if emit_state else (res[0], None)


def _rope_tables(n):
    rows = n // GRID_W
    row = jnp.repeat(jnp.arange(rows, dtype=F32), GRID_W)
    colp = jnp.tile(jnp.arange(GRID_W, dtype=F32), rows)
    nf = DA_HD // 4
    inv = ROPE_BASE ** (-jnp.arange(nf, dtype=F32) / nf)
    ar = row[:, None] * inv
    ac = colp[:, None] * inv
    cr, sr, cc, sc = jnp.cos(ar), jnp.sin(ar), jnp.cos(ac), jnp.sin(ac)
    cos = jnp.concatenate([cr, cr, cc, cc] * 2, axis=1)
    sin = jnp.concatenate([-sr, sr, -sc, sc] * 2, axis=1)
    return cos, sin


def _pad_halves(a, width, padded):
    zeros = jnp.zeros(a.shape[:-1] + (padded - width,), a.dtype)
    return jnp.concatenate([a[..., :width], zeros, a[..., width:], zeros], axis=-1)


def _trunk(x, mods, rows_per_mod, seq_len, P, cache):
    ctx_mode = cache is None
    depth = P['w_mod'].shape[0]
    n_attn = P['attn_w_in'].shape[0]
    kv, hgrn_state, gdn_state = None, None, None
    for i in range(depth):
        sh1, sc1, g1, sh2, sc2, g2 = mods[i]
        kind, j = i % 3, i // 3
        proj = functools.partial(norm_proj, x, P['norm_g'][i, 0], sc1, sh1,
                                 rows_per_mod=rows_per_mod)
        if kind == 0:
            qkv = proj(P['attn_w_in'][j], tn=1024, out_dtype=BF16)
            lam_init = 0.8 - 0.6 * math.exp(-0.3 * i)
            lp = P['attn_lambda'][j]
            lam = (jnp.exp(jnp.sum(lp[0] * lp[1])) - jnp.exp(jnp.sum(lp[2] * lp[3]))
                   + lam_init).reshape(1)
            if ctx_mode:
                o, kv = attn_context(qkv, lam, P['attn_subln'][j], seq_len, 1.0 - lam_init, j,
                                     n_attn, kv)
            else:
                o = attn_latent(qkv, cache['attn_k'], cache['attn_v'], j, cache['cos'],
                                cache['sin'], lam, P['attn_subln'][j], seq_len, 1.0 - lam_init)
            w_out = P['attn_w_out'][j]
        elif kind == 1:
            o, st = hgrn_scan(proj(P['hgrn_w_in'][j], tn=1024, out_dtype=BF16), P['lb_all'][i],
                              P['hgrn_norm'][j], P['consts'], seq_len,
                              None if ctx_mode else cache['hgrn'], j, ctx_mode)
            if ctx_mode:
                hgrn_state = st
            w_out = P['hgrn_w_out'][j]
        else:
            o, st = gdn_scan(proj(P['gdn_w_main'][j], tn=1024, out_dtype=BF16),
                             proj(P['gdn_w_tail'][j], tn=HEAD_DIM, out_dtype=F32),
                             P['gdn_conv'][j], P['gdn_rate'][j], P['gdn_bias'][j],
                             P['gdn_norm'][j], P['consts'], seq_len,
                             None if ctx_mode else cache['gdn'], j, ctx_mode)
            if ctx_mode:
                gdn_state = st
            w_out = P['gdn_w_out'][j]
        x = out_proj_residual(o, w_out, x, g1, rows_per_mod)
        x = conv_ffn_residual(x, P['norm_g'][i, 1], sc2, sh2, g2, P['ffn_w_up'][i],
                              P['ffn_conv'][i], P['ffn_conv_b'][i], P['ffn_w_down'][i],
                              P['final_g'], rows_per_mod, seq_len, i == depth - 1)
    return x, (kv, hgrn_state, gdn_state)


def kernel(x_prompt, x_sample, cache_attn_k, cache_attn_v, state_hgrn, state_gdn, c, c_ctx, norm_g, w_mod, b_mod, final_g, attn_w_in, attn_lambda, attn_subln, attn_w_out, hgrn_w_in, hgrn_lb, hgrn_norm, hgrn_w_out, gdn_w_in, gdn_conv, gdn_a_log, gdn_dt_bias, gdn_norm, gdn_w_out, ffn_w_up, ffn_conv, ffn_conv_b, ffn_w_down):
    batch, seq, d = x_prompt.shape
    dec_batch, dec_seq, _ = x_sample.shape
    depth = w_mod.shape[0]
    n_gdn = gdn_w_in.shape[0]

    lb_all = jnp.cumsum(jax.nn.softmax(hgrn_lb, axis=0), axis=0)
    lb_all = lb_all - lb_all[0]

    d_ff = ffn_w_down.shape[1]
    step_cols = FF_TILE * FFN_TILES_PER_STEP
    d_ff_pad = -(-d_ff // step_cols) * step_cols

    pad = jnp.zeros((n_gdn, HEAD_DIM - 2 * N_HEADS), F32)
    gdn_rate = jnp.concatenate([jnp.exp(gdn_a_log).reshape(n_gdn, 2 * N_HEADS), pad], axis=1)
    gdn_bias = jnp.concatenate([gdn_dt_bias.reshape(n_gdn, 2 * N_HEADS), pad], axis=1)
    tail = gdn_w_in[:, :, 4 * d:]
    tail = jnp.concatenate([tail, jnp.zeros((n_gdn, d, HEAD_DIM - tail.shape[2]), F32)], axis=2)

    P = dict(norm_g=norm_g, w_mod=w_mod, final_g=final_g,
             attn_w_in=attn_w_in.astype(BF16), attn_lambda=attn_lambda, attn_subln=attn_subln,
             attn_w_out=attn_w_out.astype(BF16),
             hgrn_w_in=hgrn_w_in.astype(BF16), lb_all=lb_all, hgrn_norm=hgrn_norm,
             hgrn_w_out=hgrn_w_out.astype(BF16),
             gdn_w_main=gdn_w_in[:, :, :4 * d].astype(BF16), gdn_w_tail=tail.astype(BF16),
             gdn_conv=gdn_conv, gdn_rate=gdn_rate.reshape(n_gdn, 1, HEAD_DIM),
             gdn_bias=gdn_bias.reshape(n_gdn, 1, HEAD_DIM), gdn_norm=gdn_norm,
             gdn_w_out=gdn_w_out.astype(BF16),
             ffn_w_up=_pad_halves(ffn_w_up.astype(BF16), d_ff, d_ff_pad),
             ffn_conv=_pad_halves(ffn_conv, d_ff, d_ff_pad),
             ffn_conv_b=_pad_halves(ffn_conv_b, d_ff, d_ff_pad),
             ffn_w_down=jnp.pad(ffn_w_down.astype(BF16), ((0, 0), (0, d_ff_pad - d_ff), (0, 0))),
             consts=_scan_constants())

    cond = jnp.zeros((16, d), F32).at[:dec_batch].set(c).at[dec_batch].set(c_ctx)
    mod = modulation_all(cond, w_mod, b_mod).reshape(depth, 16, 6, 1, d)
    mods_lat = [[mod[i, :dec_batch, t] for t in range(6)] for i in range(depth)]
    mods_ctx = [[mod[i, dec_batch:dec_batch + 1, t] for t in range(6)] for i in range(depth)]

    y_prompt, ((new_k, new_v), new_hgrn, new_gdn) = _trunk(
        x_prompt.reshape(batch * seq, d), mods_ctx, batch * seq, seq, P, None)

    cos, sin = _rope_tables(dec_seq)
    flat_k, flat_v = flatten_cache(cache_attn_k, cache_attn_v)
    cache = dict(attn_k=flat_k, attn_v=flat_v, hgrn=state_hgrn, gdn=state_gdn, cos=cos, sin=sin)
    y_sample, _ = _trunk(x_sample.reshape(dec_batch * dec_seq, d), mods_lat, dec_seq, dec_seq, P,
                         cache)

    return (y_prompt.reshape(batch, seq, d), y_sample.reshape(dec_batch, dec_seq, d),
            new_k, new_v, new_hgrn, new_gdn)
```
